```python
import math
import jax
import jax.numpy as jnp
from jax import lax
import numpy as np


D_MODEL = 1024
BATCH = 8
SEQ = 4096
DEPTH = 4

N_MIXERS = 3
ROPE_THETA = 500000.0
EPS = 1e-6
Q_BLOCK = 128
MASK_VALUE = -1e30

A_HEADS = 8
A_HEAD_DIM = D_MODEL // A_HEADS // 2
A_V_DIM = 2 * A_HEAD_DIM
A_ROT = A_HEAD_DIM // 4

B_HEADS = 16
B_Q_RANK = 384
B_KV_RANK = 256
B_NOPE = 64
B_ROPE = 32
B_V = 64

C_GROUPS = ((128, 1), (512, 4), (2048, 16))
C_HEADS = 16
C_HEAD_DIM = 64
C_ROT = C_HEAD_DIM // 4
C_BLOCK = 64

FFN_DIM = -(-8 * D_MODEL // (3 * 256)) * 256

N_A = (DEPTH + 2) // 3
N_B = (DEPTH + 1) // 3
N_C = DEPTH // 3

kernel_name = 'hybrid_interleaved_diff_mla_dilated_encoder'


def rms_norm(x, g):
    xf = x.astype(jnp.float32)
    y = xf * lax.rsqrt(jnp.mean(xf * xf, axis=-1, keepdims=True) + EPS) * g.astype(jnp.float32)
    return y.astype(x.dtype)


def rope_tables(seq_len, rot_dim):
    pos = jnp.arange(seq_len, dtype=jnp.float32)
    inv = ROPE_THETA ** (-jnp.arange(0, rot_dim, 2, dtype=jnp.float32) / rot_dim)
    ang = pos[:, None] * inv[None, :]
    return jnp.cos(ang), jnp.sin(ang)


def apply_rope(x, cos, sin):
    r = cos.shape[-1]
    shape = (x.shape[1],) + (1,) * (x.ndim - 3) + (r,)
    c = cos.reshape(shape)
    s = sin.reshape(shape)
    xf = x[..., :2 * r].astype(jnp.float32)
    x1, x2 = xf[..., :r], xf[..., r:]
    rot = jnp.concatenate([x1 * c - x2 * s, x2 * c + x1 * s], axis=-1).astype(x.dtype)
    return jnp.concatenate([rot, x[..., 2 * r:]], axis=-1)


def _to_blocks(t):
    b, h, s, d = t.shape
    return t.reshape(b, h, s // Q_BLOCK, Q_BLOCK, d).transpose(2, 0, 1, 3, 4)


def _from_blocks(t):
    nb, b, h, q, d = t.shape
    return t.transpose(1, 2, 0, 3, 4).reshape(b, h, nb * q, d)


def diff_attention(h, w_qkv, lam_q1, lam_k1, lam_q2, lam_k2, subln, w_o, lambda_init, cos, sin):
    B, S, _ = h.shape
    qk_w = 2 * A_HEADS * A_HEAD_DIM
    qkv = h @ w_qkv
    q = qkv[..., :qk_w].reshape(B, S, 2 * A_HEADS, A_HEAD_DIM)
    k = qkv[..., qk_w:2 * qk_w].reshape(B, S, 2 * A_HEADS, A_HEAD_DIM)
    v = qkv[..., 2 * qk_w:].reshape(B, S, A_HEADS, A_V_DIM)
    q = (apply_rope(q, cos, sin) * (A_HEAD_DIM ** -0.5)).transpose(0, 2, 1, 3)
    k = apply_rope(k, cos, sin).transpose(0, 2, 1, 3)
    v = v.transpose(0, 2, 1, 3)
    lam = (jnp.exp(jnp.sum(lam_q1.astype(jnp.float32) * lam_k1.astype(jnp.float32)))
           - jnp.exp(jnp.sum(lam_q2.astype(jnp.float32) * lam_k2.astype(jnp.float32)))
           + lambda_init)

    def block(qb):
        s = jnp.einsum('bhqd,bhkd->bhqk', qb, k).astype(jnp.float32)
        p = jax.nn.softmax(s, axis=-1).reshape(B, A_HEADS, 2, qb.shape[2], S)
        a = p[:, :, 0] - lam * p[:, :, 1]
        return jnp.einsum('bhqk,bhkd->bhqd', a.astype(v.dtype), v)

    o = _from_blocks(lax.map(block, _to_blocks(q)))
    o = rms_norm(o, subln) * (1.0 - lambda_init)
    o = o.transpose(0, 2, 1, 3).reshape(B, S, A_HEADS * A_V_DIM)
    return o @ w_o


def latent_attention(h, w_a, q_norm, kv_norm, w_qb, w_kvb, w_o, cos, sin):
    B, S, _ = h.shape
    a = h @ w_a
    q_lat = rms_norm(a[..., :B_Q_RANK], q_norm)
    kv_lat = rms_norm(a[..., B_Q_RANK:B_Q_RANK + B_KV_RANK], kv_norm)
    k_rope = apply_rope(a[..., B_Q_RANK + B_KV_RANK:], cos, sin)
    scale = (B_NOPE + B_ROPE) ** -0.5
    q = (q_lat @ w_qb).reshape(B, S, B_HEADS, B_NOPE + B_ROPE)
    q_nope = (q[..., :B_NOPE] * scale).transpose(0, 2, 1, 3)
    q_rope = (apply_rope(q[..., B_NOPE:], cos, sin) * scale).transpose(0, 2, 1, 3)
    kv = (kv_lat @ w_kvb).reshape(B, S, B_HEADS, B_NOPE + B_V)
    k_nope = kv[..., :B_NOPE].transpose(0, 2, 1, 3)
    v = kv[..., B_NOPE:].transpose(0, 2, 1, 3)

    def block(qs):
        qn, qr = qs
        s = (jnp.einsum('bhqd,bhkd->bhqk', qn, k_nope)
             + jnp.einsum('bhqr,bkr->bhqk', qr, k_rope)).astype(jnp.float32)
        p = jax.nn.softmax(s, axis=-1)
        return jnp.einsum('bhqk,bhkd->bhqd', p.astype(v.dtype), v)

    o = _from_blocks(lax.map(block, (_to_blocks(q_nope), _to_blocks(q_rope))))
    o = o.transpose(0, 2, 1, 3).reshape(B, S, B_HEADS * B_V)
    return o @ w_o


def _dilated_group(q, k, v, window, dilation):
    B, S, H, hd = q.shape
    radius = (window // 2) // dilation
    L = S // dilation
    nb = -(-L // C_BLOCK)
    Lp = nb * C_BLOCK

    def phase(t):
        t = t.reshape(B, L, dilation, H, hd).transpose(0, 2, 3, 1, 4)
        return jnp.pad(t, ((0, 0), (0, 0), (0, 0), (0, Lp - L), (0, 0)))

    def band(t):
        t = jnp.pad(phase(t), ((0, 0), (0, 0), (0, 0), (C_BLOCK, C_BLOCK), (0, 0)))
        t = t.reshape(B, dilation, H, nb + 2, C_BLOCK, hd)
        return jnp.concatenate([t[:, :, :, :-2], t[:, :, :, 1:-1], t[:, :, :, 2:]], axis=-2)

    qb = phase(q).reshape(B, dilation, H, nb, C_BLOCK, hd)
    kb, vb = band(k), band(v)
    blk = jnp.arange(nb)[:, None]
    qpos = blk * C_BLOCK + jnp.arange(C_BLOCK)[None, :]
    kpos = (blk - 1) * C_BLOCK + jnp.arange(3 * C_BLOCK)[None, :]
    mask = ((jnp.abs(qpos[:, :, None] - kpos[:, None, :]) <= radius)
            & (kpos[:, None, :] >= 0) & (kpos[:, None, :] < L))
    s = jnp.einsum('bphnqd,bphnkd->bphnqk', qb, kb).astype(jnp.float32)
    s = jnp.where(mask, s, MASK_VALUE)
    lse = jax.nn.logsumexp(s, axis=-1, keepdims=True)
    p = jnp.exp(s - lse)
    o = jnp.einsum('bphnqk,bphnkd->bphnqd', p.astype(v.dtype), vb)
    o = o.reshape(B, dilation, H, Lp, hd)[:, :, :, :L].transpose(0, 3, 1, 2, 4).reshape(B, S, H, hd)
    lse = lse[..., 0].reshape(B, dilation, H, Lp)[..., :L].transpose(0, 3, 1, 2).reshape(B, S, H)
    return o, lse


def dilated_attention(h, w_qkv, w_o, cos, sin):
    B, S, _ = h.shape
    G = len(C_GROUPS)
    qkv = (h @ w_qkv).reshape(B, S, G, 3, C_HEADS, C_HEAD_DIM)
    q = apply_rope(qkv[:, :, :, 0], cos, sin) * (C_HEAD_DIM ** -0.5)
    k = apply_rope(qkv[:, :, :, 1], cos, sin)
    v = qkv[:, :, :, 2]
    outs, lses = [], []
    for g, (window, dilation) in enumerate(C_GROUPS):
        o_g, l_g = _dilated_group(q[:, :, g], k[:, :, g], v[:, :, g], window, dilation)
        outs.append(o_g)
        lses.append(l_g)
    wts = jax.nn.softmax(jnp.stack(lses, axis=0), axis=0)
    o = jnp.sum(wts[..., None].astype(v.dtype) * jnp.stack(outs, axis=0), axis=0)
    return o.reshape(B, S, C_HEADS * C_HEAD_DIM) @ w_o


def swiglu(h, w_gu, w_out):
    gu = h @ w_gu
    return (jax.nn.silu(gu[..., :FFN_DIM]) * gu[..., FFN_DIM:]) @ w_out


def _w(k, shape, fan_in):
    return jax.random.normal(k, shape, jnp.float32) * (fan_in ** -0.5)


def _gain(k, shape):
    return 1.0 + 0.02 * jax.random.normal(k, shape, jnp.float32)


def setup_inputs(seed: int = 0) -> dict:
    key = jax.random.key(seed)
    ks = jax.random.split(key, 21)
    D = D_MODEL
    a_qkv = 2 * 2 * A_HEADS * A_HEAD_DIM + A_HEADS * A_V_DIM
    c_qkv = len(C_GROUPS) * 3 * C_HEADS * C_HEAD_DIM
    return {
        'x': jax.random.normal(ks[0], (BATCH, SEQ, D), jnp.float32),
        'attn_norm': _gain(ks[1], (DEPTH, D)),
        'ffn_norm': _gain(ks[2], (DEPTH, D)),
        'final_norm': _gain(ks[3], (D,)),
        'a_w_qkv': _w(ks[4], (N_A, D, a_qkv), D),
        'a_lambda_q1': 0.1 * jax.random.normal(ks[5], (N_A, A_HEAD_DIM), jnp.float32),
        'a_lambda_k1': 0.1 * jax.random.normal(ks[6], (N_A, A_HEAD_DIM), jnp.float32),
        'a_lambda_q2': 0.1 * jax.random.normal(ks[7], (N_A, A_HEAD_DIM), jnp.float32),
        'a_lambda_k2': 0.1 * jax.random.normal(ks[8], (N_A, A_HEAD_DIM), jnp.float32),
        'a_subln': _gain(ks[9], (N_A, A_V_DIM)),
        'a_w_o': _w(ks[10], (N_A, A_HEADS * A_V_DIM, D), A_HEADS * A_V_DIM),
        'b_w_a': _w(ks[11], (N_B, D, B_Q_RANK + B_KV_RANK + B_ROPE), D),
        'b_q_norm': _gain(ks[12], (N_B, B_Q_RANK)),
        'b_kv_norm': _gain(ks[13], (N_B, B_KV_RANK)),
        'b_w_qb': _w(ks[14], (N_B, B_Q_RANK, B_HEADS * (B_NOPE + B_ROPE)), B_Q_RANK),
        'b_w_kvb': _w(ks[15], (N_B, B_KV_RANK, B_HEADS * (B_NOPE + B_V)), B_KV_RANK),
        'b_w_o': _w(ks[16], (N_B, B_HEADS * B_V, D), B_HEADS * B_V),
        'c_w_qkv': _w(ks[17], (N_C, D, c_qkv), D),
        'c_w_o': _w(ks[18], (N_C, C_HEADS * C_HEAD_DIM, D), C_HEADS * C_HEAD_DIM),
        'f_w_gu': _w(ks[19], (DEPTH, D, 2 * FFN_DIM), D),
        'f_w_out': _w(ks[20], (DEPTH, FFN_DIM, D), FFN_DIM),
    }


def reference(x, attn_norm, ffn_norm, final_norm, a_w_qkv, a_lambda_q1, a_lambda_k1, a_lambda_q2,
              a_lambda_k2, a_subln, a_w_o, b_w_a, b_q_norm, b_kv_norm, b_w_qb, b_w_kvb, b_w_o,
              c_w_qkv, c_w_o, f_w_gu, f_w_out):
    S = x.shape[1]
    cos_p, sin_p = rope_tables(S, A_ROT)
    cos_b, sin_b = rope_tables(S, B_ROPE)
    for i in range(DEPTH):
        m, j = i % N_MIXERS, i // N_MIXERS
        h = rms_norm(x, attn_norm[i])
        if m == 0:
            lambda_init = 0.8 - 0.6 * math.exp(-0.3 * i)
            mix = diff_attention(h, a_w_qkv[j], a_lambda_q1[j], a_lambda_k1[j], a_lambda_q2[j],
                                 a_lambda_k2[j], a_subln[j], a_w_o[j], lambda_init, cos_p, sin_p)
        elif m == 1:
            mix = latent_attention(h, b_w_a[j], b_q_norm[j], b_kv_norm[j], b_w_qb[j], b_w_kvb[j],
                                   b_w_o[j], cos_b, sin_b)
        else:
            mix = dilated_attention(h, c_w_qkv[j], c_w_o[j], cos_p, sin_p)
        x = x + mix
        x = x + swiglu(rms_norm(x, ffn_norm[i]), f_w_gu[i], f_w_out[i])
    return rms_norm(x, final_norm)
```

```python
import functools
import math

import jax
import jax.numpy as jnp
from jax import lax
from jax.experimental import pallas as pl
from jax.experimental.pallas import tpu as pltpu

F32 = jnp.float32
BF16 = jnp.bfloat16

D_MODEL = 1024
N_MIXERS = 3
ROPE_THETA = 500000.0
EPS = 1e-6
NEG_BIG = -1e30
LOG2E = 1.4426950408889634

A_HEADS = 8
A_HEAD_DIM = 64
A_V_DIM = 128
A_ROT = 16

B_HEADS = 16
B_Q_RANK = 384
B_KV_RANK = 256
B_NOPE = 64
B_ROPE = 32
B_V = 64

C_GROUPS = ((128, 1), (512, 4), (2048, 16))
C_HEADS = 16
C_HEAD_DIM = 64
C_ROT = 16
C_HALO = 64

FFN_DIM = 2816
FFN_CHUNK = 256

LANES = 128
ONES_ROWS = 16
VMEM_LIMIT_BYTES = 56 * 1024 * 1024

NT_DIMS = (((1,), (1,)), ((), ()))


def _tile(n, pref):
    t = min(n, pref)
    assert n % t == 0, (n, t)
    return t


def _params(sem):
    return pltpu.CompilerParams(dimension_semantics=sem, vmem_limit_bytes=VMEM_LIMIT_BYTES)


def _const_spec(shape):
    nd = len(shape)
    return pl.BlockSpec(shape, lambda *_: (0,) * nd, pipeline_mode=pl.Buffered(1))


def _rms_rows(x, gain):
    return x * lax.rsqrt(jnp.mean(x * x, axis=-1, keepdims=True) + EPS) * gain


def _rope_tables(seq_len, rot_dim):
    pos = jnp.arange(seq_len, dtype=F32)
    inv = ROPE_THETA ** (-jnp.arange(0, rot_dim, 2, dtype=F32) / rot_dim)
    ang = pos[:, None] * inv[None, :]
    return jnp.cos(ang), jnp.sin(ang)


def _lane_tables(cos, sin, period, offset, scale):
    s, r = cos.shape
    c = jnp.ones((s, period), F32)
    c = c.at[:, offset:offset + r].set(cos).at[:, offset + r:offset + 2 * r].set(cos)
    sa = jnp.zeros((s, period), F32).at[:, offset:offset + r].set(-sin)
    sb = jnp.zeros((s, period), F32).at[:, offset + r:offset + 2 * r].set(sin)
    rep = LANES // period
    return jnp.stack([jnp.tile(t, (1, rep)) for t in (c, sa, sb)]) * scale


def _rope(x, tab_ref, base, r):
    return (x * tab_ref[base]
            + pltpu.roll(x, LANES - r, 1) * tab_ref[base + 1]
            + pltpu.roll(x, r, 1) * tab_ref[base + 2])


def _softmax_t(qm, k_ref, k_lanes, vt_ref, vt_head, tk):
    tq = qm.shape[0]
    nchunk = vt_ref.shape[len(vt_head)]
    rows = vt_ref.shape[-2]

    def step(c, carry):
        m, acc = carry
        start = pl.multiple_of(c * tk, tk)
        ks = k_ref[pl.ds(start, tk), k_lanes]
        st = lax.dot_general(ks, qm, NT_DIMS, preferred_element_type=F32)
        m_new = jnp.maximum(m, jnp.max(st, axis=0, keepdims=True))
        alpha = jnp.exp2(m - m_new)
        pt = jnp.exp2(st - m_new).astype(BF16)
        pv = jnp.dot(vt_ref[vt_head + (c,)], pt, preferred_element_type=F32)
        return m_new, acc * alpha + pv

    init = (jnp.full((1, tq), NEG_BIG, F32), jnp.zeros((rows, tq), F32))
    _, acc = lax.fori_loop(0, nchunk, step, init)
    return acc


def _proj_diff_kernel(x_ref, g_ref, wqk_ref, wvt_ref, tab_ref, q_ref, k_ref, vt_ref, *, tk):
    tm = x_ref.shape[0]
    h = _rms_rows(x_ref[...], g_ref[...]).astype(BF16)
    for c in range(4):
        y = jnp.dot(h, wqk_ref[:, c * 512:(c + 1) * 512], preferred_element_type=F32)
        dst, base = (q_ref, 0) if c < 2 else (k_ref, 3)
        for j in range(4):
            lo = (c % 2) * 512 + j * LANES
            dst[:, lo:lo + LANES] = _rope(y[:, j * LANES:(j + 1) * LANES], tab_ref, base,
                                          A_ROT // 2).astype(BF16)
    for half in range(2):
        vt = lax.dot_general(wvt_ref[half * 512:(half + 1) * 512, :], h, NT_DIMS,
                             preferred_element_type=F32)
        for hh in range(4):
            hv = half * 4 + hh
            for c in range(tm // tk):
                vt_ref[hv, c, 0:A_V_DIM, :] = vt[hh * A_V_DIM:(hh + 1) * A_V_DIM,
                                                 c * tk:(c + 1) * tk].astype(BF16)
                vt_ref[hv, c, A_V_DIM:A_V_DIM + ONES_ROWS, :] = jnp.ones((ONES_ROWS, tk), BF16)


def _attn_diff_kernel(lam_ref, sub_ref, q_ref, k_ref, vt_ref, o_ref, *, tk, lambda_init):
    q2 = q_ref[...]
    low = lax.broadcasted_iota(jnp.int32, q2.shape, 1) < A_HEAD_DIM
    zero = jnp.zeros_like(q2)
    outs = []
    for mp in range(2):
        qm = jnp.where(low, q2, zero) if mp == 0 else jnp.where(low, zero, q2)
        acc = _softmax_t(qm, k_ref, slice(None), vt_ref, (), tk)
        outs.append(acc[0:A_V_DIM] * (1.0 / acc[A_V_DIM:A_V_DIM + 1]))
    l1 = jnp.sum(lam_ref[0:1, :] * lam_ref[1:2, :], axis=-1, keepdims=True)
    l2 = jnp.sum(lam_ref[2:3, :] * lam_ref[3:4, :], axis=-1, keepdims=True)
    lam = jnp.exp(l1) - jnp.exp(l2) + lambda_init
    o = (outs[0] - lam * outs[1]).T
    o_ref[...] = (_rms_rows(o, sub_ref[...]) * (1.0 - lambda_init)).astype(o_ref.dtype)


def _diff_attention(x, gain, w_qkv, lam_q1, lam_k1, lam_q2, lam_k2, subln, lambda_init, cos, sin):
    b, s, d = x.shape
    tm = _tile(s, 512)
    tq = _tile(s, 512)
    tk = _tile(s, 512)
    tm = max(tm, tk)
    nck = s // tk
    qk_w = 2 * A_HEADS * A_HEAD_DIM
    scale = A_HEAD_DIM ** -0.5 * LOG2E
    tab = jnp.concatenate([_lane_tables(cos, sin, A_HEAD_DIM, 0, scale),
                           _lane_tables(cos, sin, A_HEAD_DIM, 0, 1.0)])
    wqk = w_qkv[:, :2 * qk_w].astype(BF16)
    wvt = w_qkv[:, 2 * qk_w:].T.astype(BF16)
    vrows = A_V_DIM + ONES_ROWS

    q, k, vt = pl.pallas_call(
        functools.partial(_proj_diff_kernel, tk=tk),
        grid=(b, s // tm),
        in_specs=[
            pl.BlockSpec((None, tm, d), lambda bi, i: (bi, i, 0)),
            _const_spec((1, d)),
            _const_spec((d, 2 * qk_w)),
            _const_spec((A_HEADS * A_V_DIM, d)),
            pl.BlockSpec((6, tm, LANES), lambda bi, i: (0, i, 0)),
        ],
        out_specs=[
            pl.BlockSpec((None, tm, qk_w), lambda bi, i: (bi, i, 0)),
            pl.BlockSpec((None, tm, qk_w), lambda bi, i: (bi, i, 0)),
            pl.BlockSpec((None, A_HEADS, tm // tk, vrows, tk), lambda bi, i: (bi, 0, i, 0, 0)),
        ],
        out_shape=[
            jax.ShapeDtypeStruct((b, s, qk_w), BF16),
            jax.ShapeDtypeStruct((b, s, qk_w), BF16),
            jax.ShapeDtypeStruct((b, A_HEADS, nck, vrows, tk), BF16),
        ],
        compiler_params=_params(("parallel", "parallel")),
        name="proj_diff",
    )(x, gain.reshape(1, d), wqk, wvt, tab)

    lam = jnp.stack([lam_q1, lam_k1, lam_q2, lam_k2]).astype(F32)
    return pl.pallas_call(
        functools.partial(_attn_diff_kernel, tk=tk, lambda_init=lambda_init),
        grid=(b, A_HEADS, s // tq),
        in_specs=[
            _const_spec((4, A_HEAD_DIM)),
            _const_spec((1, A_V_DIM)),
            pl.BlockSpec((None, tq, LANES), lambda bi, h, i: (bi, i, h)),
            pl.BlockSpec((None, s, LANES), lambda bi, h, i: (bi, 0, h)),
            pl.BlockSpec((None, None, nck, vrows, tk), lambda bi, h, i: (bi, h, 0, 0, 0)),
        ],
        out_specs=pl.BlockSpec((None, tq, A_V_DIM), lambda bi, h, i: (bi, i, h)),
        out_shape=jax.ShapeDtypeStruct((b, s, A_HEADS * A_V_DIM), BF16),
        compiler_params=_params(("parallel", "parallel", "arbitrary")),
        name="attn_diff",
    )(lam, subln.reshape(1, A_V_DIM).astype(F32), q, k, vt)


def _proj_mla_kernel(x_ref, g_ref, waq_ref, wakv_ref, war_ref, qn_ref, kvn_ref, wqb_ref, wkb_ref,
                     wvt_ref, tab_ref, q_ref, k_ref, vt_ref, *, tk):
    tm = x_ref.shape[0]
    r = B_ROPE // 2
    h = _rms_rows(x_ref[...], g_ref[...]).astype(BF16)
    q_lat = _rms_rows(jnp.dot(h, waq_ref[...], preferred_element_type=F32), qn_ref[...]).astype(BF16)
    kv_lat = _rms_rows(jnp.dot(h, wakv_ref[...], preferred_element_type=F32), kvn_ref[...]).astype(BF16)
    k_rope = _rope(jnp.dot(h, war_ref[...], preferred_element_type=F32), tab_ref, 3, r)
    for c in range(4):
        yq = jnp.dot(q_lat, wqb_ref[:, c * 512:(c + 1) * 512], preferred_element_type=F32)
        yk = jnp.dot(kv_lat, wkb_ref[:, c * 512:(c + 1) * 512], preferred_element_type=F32)
        for j in range(4):
            lo = c * 512 + j * LANES
            q_ref[:, lo:lo + LANES] = _rope(yq[:, j * LANES:(j + 1) * LANES], tab_ref, 0, r).astype(BF16)
            k_ref[:, lo:lo + LANES] = (yk[:, j * LANES:(j + 1) * LANES] + k_rope).astype(BF16)
    for half in range(2):
        vt = lax.dot_general(wvt_ref[half * 512:(half + 1) * 512, :], kv_lat, NT_DIMS,
                             preferred_element_type=F32)
        for hh in range(8):
            hv = half * 8 + hh
            for c in range(tm // tk):
                vt_ref[hv, c, 0:B_V, :] = vt[hh * B_V:(hh + 1) * B_V, c * tk:(c + 1) * tk].astype(BF16)
                vt_ref[hv, c, B_V:B_V + ONES_ROWS, :] = jnp.ones((ONES_ROWS, tk), BF16)


def _attn_mla_kernel(q_ref, k_ref, vt_ref, o_ref, *, tk):
    outs = []
    for e in range(2):
        lanes = slice(e * LANES, (e + 1) * LANES)
        acc = _softmax_t(q_ref[:, lanes], k_ref, lanes, vt_ref, (e,), tk)
        outs.append(acc[0:B_V] * (1.0 / acc[B_V:B_V + 1]))
    o_ref[...] = jnp.concatenate(outs, axis=0).T.astype(o_ref.dtype)


def _latent_attention(x, gain, w_a, q_norm, kv_norm, w_qb, w_kvb, cos, sin):
    b, s, d = x.shape
    tq = _tile(s, 512)
    tk = _tile(s, 512)
    tm = tk
    nck = s // tk
    hw = B_HEADS * LANES
    scale = (B_NOPE + B_ROPE) ** -0.5 * LOG2E
    tab = jnp.concatenate([_lane_tables(cos, sin, LANES, B_NOPE, scale),
                           _lane_tables(cos, sin, LANES, B_NOPE, 1.0)])
    waq = w_a[:, :B_Q_RANK].astype(BF16)
    wakv = w_a[:, B_Q_RANK:B_Q_RANK + B_KV_RANK].astype(BF16)
    war = jnp.zeros((d, LANES), F32).at[:, B_NOPE:B_NOPE + B_ROPE].set(
        w_a[:, B_Q_RANK + B_KV_RANK:]).astype(BF16)
    pad = LANES - (B_NOPE + B_ROPE)
    wqb = jnp.pad(w_qb.reshape(B_Q_RANK, B_HEADS, B_NOPE + B_ROPE),
                  ((0, 0), (0, 0), (0, pad))).reshape(B_Q_RANK, hw).astype(BF16)
    wkv3 = w_kvb.reshape(B_KV_RANK, B_HEADS, B_NOPE + B_V)
    wkb = jnp.pad(wkv3[:, :, :B_NOPE], ((0, 0), (0, 0), (0, LANES - B_NOPE))
                  ).reshape(B_KV_RANK, hw).astype(BF16)
    wvt = wkv3[:, :, B_NOPE:].reshape(B_KV_RANK, B_HEADS * B_V).T.astype(BF16)
    vrows = B_V + ONES_ROWS

    q, k, vt = pl.pallas_call(
        functools.partial(_proj_mla_kernel, tk=tk),
        grid=(b, s // tm),
        in_specs=[
            pl.BlockSpec((None, tm, d), lambda bi, i: (bi, i, 0)),
            _const_spec((1, d)),
            _const_spec((d, B_Q_RANK)),
            _const_spec((d, B_KV_RANK)),
            _const_spec((d, LANES)),
            _const_spec((1, B_Q_RANK)),
            _const_spec((1, B_KV_RANK)),
            _const_spec((B_Q_RANK, hw)),
            _const_spec((B_KV_RANK, hw)),
            _const_spec((B_HEADS * B_V, B_KV_RANK)),
            pl.BlockSpec((6, tm, LANES), lambda bi, i: (0, i, 0)),
        ],
        out_specs=[
            pl.BlockSpec((None, tm, hw), lambda bi, i: (bi, i, 0)),
            pl.BlockSpec((None, tm, hw), lambda bi, i: (bi, i, 0)),
            pl.BlockSpec((None, B_HEADS, tm // tk, vrows, tk), lambda bi, i: (bi, 0, i, 0, 0)),
        ],
        out_shape=[
            jax.ShapeDtypeStruct((b, s, hw), BF16),
            jax.ShapeDtypeStruct((b, s, hw), BF16),
            jax.ShapeDtypeStruct((b, B_HEADS, nck, vrows, tk), BF16),
        ],
        compiler_params=_params(("parallel", "parallel")),
        name="proj_mla",
    )(x, gain.reshape(1, d), waq, wakv, war, q_norm.reshape(1, -1), kv_norm.reshape(1, -1),
      wqb, wkb, wvt, tab)

    return pl.pallas_call(
        functools.partial(_attn_mla_kernel, tk=tk),
        grid=(b, B_HEADS // 2, s // tq),
        in_specs=[
            pl.BlockSpec((None, tq, 2 * LANES), lambda bi, h, i: (bi, i, h)),
            pl.BlockSpec((None, s, 2 * LANES), lambda bi, h, i: (bi, 0, h)),
            pl.BlockSpec((None, 2, nck, vrows, tk), lambda bi, h, i: (bi, h, 0, 0, 0)),
        ],
        out_specs=pl.BlockSpec((None, tq, 2 * B_V), lambda bi, h, i: (bi, i, h)),
        out_shape=jax.ShapeDtypeStruct((b, s, B_HEADS * B_V), BF16),
        compiler_params=_params(("parallel", "parallel", "arbitrary")),
        name="attn_mla",
    )(q, k, vt)


def _proj_dil_kernel(x_ref, g_ref, w_ref, tab_ref, q_ref, k_ref, v_ref):
    h = _rms_rows(x_ref[...], g_ref[...]).astype(BF16)
    width = C_HEADS * C_HEAD_DIM
    for c in range(3 * width // 512):
        y = jnp.dot(h, w_ref[:, c * 512:(c + 1) * 512], preferred_element_type=F32)
        which = c // 2
        dst = (q_ref, k_ref, v_ref)[which]
        for j in range(4):
            lo = (c % 2) * 512 + j * LANES
            yj = y[:, j * LANES:(j + 1) * LANES]
            if which < 2:
                yj = _rope(yj, tab_ref, 3 * which, C_ROT // 2)
            dst[:, lo:lo + LANES] = yj.astype(BF16)


def _attn_dil_kernel(q_ref, kp_ref, kc_ref, kn_ref, vp_ref, vc_ref, vn_ref, o_ref, lse_ref, *, seq):
    tq = q_ref.shape[0]
    w = tq + 2 * C_HALO
    q0 = pl.program_id(2) * tq
    qpos = q0 + lax.broadcasted_iota(jnp.int32, (tq, w), 0)
    kpos = q0 - C_HALO + lax.broadcasted_iota(jnp.int32, (tq, w), 1)
    mask = (jnp.abs(qpos - kpos) <= C_HALO) & (kpos >= 0) & (kpos < seq)
    low = lax.broadcasted_iota(jnp.int32, (tq, LANES), 1) < C_HEAD_DIM
    head_lane = lax.broadcasted_iota(jnp.int32, (tq, LANES), 1)
    lse_all = jnp.zeros((tq, LANES), F32)
    for j in range(C_HEADS // 2):
        lanes = slice(j * LANES, (j + 1) * LANES)
        q2 = q_ref[:, lanes]
        zero = jnp.zeros_like(q2)
        k2 = jnp.concatenate([kp_ref[:, lanes], kc_ref[:, lanes], kn_ref[:, lanes]], axis=0)
        v2 = jnp.concatenate([vp_ref[:, lanes], vc_ref[:, lanes], vn_ref[:, lanes]], axis=0)
        halves = []
        for e in range(2):
            qm = jnp.where(low, q2, zero) if e == 0 else jnp.where(low, zero, q2)
            sc = lax.dot_general(qm, k2, NT_DIMS, preferred_element_type=F32)
            sc = jnp.where(mask, sc, NEG_BIG)
            m = jnp.max(sc, axis=-1, keepdims=True)
            p = jnp.exp2(sc - m)
            l = jnp.sum(p, axis=-1, keepdims=True)
            pv = jnp.dot(p.astype(BF16), v2, preferred_element_type=F32)
            halves.append(pv * (1.0 / l))
            lse_all = jnp.where(head_lane == 2 * j + e, m + jnp.log2(l), lse_all)
        o_ref[:, lanes] = jnp.where(low, halves[0], halves[1]).astype(o_ref.dtype)
    lse_ref[...] = lse_all


def _combine_dil_kernel(o1_ref, o2_ref, o3_ref, l1_ref, l2_ref, l3_ref, o_ref):
    ls = [l1_ref[...], l2_ref[...], l3_ref[...]]
    m = jnp.maximum(jnp.maximum(ls[0], ls[1]), ls[2])
    es = [jnp.exp2(l - m) for l in ls]
    inv = 1.0 / (es[0] + es[1] + es[2])
    width = C_HEADS * C_HEAD_DIM
    expand = (lax.broadcasted_iota(jnp.int32, (LANES, width), 1) // C_HEAD_DIM
              == lax.broadcasted_iota(jnp.int32, (LANES, width), 0)).astype(BF16)
    acc = jnp.zeros(o_ref.shape, F32)
    for e, oref in zip(es, (o1_ref, o2_ref, o3_ref)):
        wt = e * inv
        hi = wt.astype(BF16)
        lo = (wt - hi.astype(F32)).astype(BF16)
        wide = (jnp.dot(hi, expand, preferred_element_type=F32)
                + jnp.dot(lo, expand, preferred_element_type=F32))
        acc = acc + wide * oref[...].astype(F32)
    o_ref[...] = acc.astype(o_ref.dtype)


def _dilated_attention(x, gain, w_qkv, cos, sin):
    b, s, d = x.shape
    width = C_HEADS * C_HEAD_DIM
    scale = C_HEAD_DIM ** -0.5 * LOG2E
    tab = jnp.concatenate([_lane_tables(cos, sin, C_HEAD_DIM, 0, scale),
                           _lane_tables(cos, sin, C_HEAD_DIM, 0, 1.0)])
    w_bf = w_qkv.astype(BF16)
    outs, lses = [], []
    for g, (window, dil) in enumerate(C_GROUPS):
        assert (window // 2) // dil == C_HALO
        seq = s // dil
        assert seq % C_HALO == 0
        tl = _tile(seq, 512)
        xg = x.reshape(b, seq, dil * d)
        tabg = tab.reshape(6, seq, dil * LANES)
        q, k, v = pl.pallas_call(
            _proj_dil_kernel,
            grid=(b, dil, seq // tl),
            in_specs=[
                pl.BlockSpec((None, tl, d), lambda bi, p, i: (bi, i, p)),
                _const_spec((1, d)),
                _const_spec((d, 3 * width)),
                pl.BlockSpec((6, tl, LANES), lambda bi, p, i: (0, i, p)),
            ],
            out_specs=[pl.BlockSpec((None, None, tl, width), lambda bi, p, i: (bi, p, i, 0))] * 3,
            out_shape=[jax.ShapeDtypeStruct((b, dil, seq, width), BF16)] * 3,
            compiler_params=_params(("parallel", "parallel", "parallel")),
            name=f"proj_dil{g}",
        )(xg, gain.reshape(1, d), w_bf[:, g * 3 * width:(g + 1) * 3 * width], tabg)

        tq = _tile(seq, 256)
        r = tq // C_HALO
        nhalo = seq // C_HALO
        cur = pl.BlockSpec((None, None, tq, width), lambda bi, p, i: (bi, p, i, 0))
        prev = pl.BlockSpec((None, None, C_HALO, width),
                            lambda bi, p, i: (bi, p, jnp.maximum(i * r - 1, 0), 0))
        nxt = pl.BlockSpec((None, None, C_HALO, width),
                           lambda bi, p, i: (bi, p, jnp.minimum((i + 1) * r, nhalo - 1), 0))
        o_g, lse_g = pl.pallas_call(
            functools.partial(_attn_dil_kernel, seq=seq),
            grid=(b, dil, seq // tq),
            in_specs=[cur, prev, cur, nxt, prev, cur, nxt],
            out_specs=[cur, pl.BlockSpec((None, None, tq, LANES), lambda bi, p, i: (bi, p, i, 0))],
            out_shape=[jax.ShapeDtypeStruct((b, dil, seq, width), BF16),
                       jax.ShapeDtypeStruct((b, dil, seq, LANES), F32)],
            compiler_params=_params(("parallel", "parallel", "parallel")),
            name=f"attn_dil{g}",
        )(q, k, k, k, v, v, v)
        outs.append(o_g)
        lses.append(lse_g)

    dmax = C_GROUPS[-1][1]
    rows = s // dmax
    ins, specs = [], []
    for cols, arrs in ((width, outs), (LANES, lses)):
        for arr, (_, dil) in zip(arrs, C_GROUPS):
            sub = dmax // dil
            ins.append(arr.reshape(b, dil, rows, sub * cols))
            specs.append(pl.BlockSpec((None, None, rows, cols),
                                      lambda bi, p, dil=dil: (bi, p % dil, 0, p // dil)))
    o = pl.pallas_call(
        _combine_dil_kernel,
        grid=(b, dmax),
        in_specs=specs,
        out_specs=pl.BlockSpec((None, rows, width), lambda bi, p: (bi, 0, p)),
        out_shape=jax.ShapeDtypeStruct((b, rows, dmax * width), BF16),
        compiler_params=_params(("parallel", "parallel")),
        name="combine_dil",
    )(*ins)
    return o.reshape(b, s, width)


def _post_kernel(o_ref, x_ref, wo_ref, g_ref, wgu_ref, wout_ref, fg_ref, y_ref, acc_ref, *,
                 apply_final):
    x1 = x_ref[...] + jnp.dot(o_ref[...], wo_ref[...], preferred_element_type=F32)
    h = _rms_rows(x1, g_ref[...]).astype(BF16)
    acc_ref[...] = x1
    for c in range(FFN_DIM // FFN_CHUNK):
        lo = c * FFN_CHUNK
        gate = jnp.dot(h, wgu_ref[:, lo:lo + FFN_CHUNK], preferred_element_type=F32)
        up = jnp.dot(h, wgu_ref[:, FFN_DIM + lo:FFN_DIM + lo + FFN_CHUNK], preferred_element_type=F32)
        act = (gate * jax.nn.sigmoid(gate) * up).astype(BF16)
        acc_ref[...] += jnp.dot(act, wout_ref[lo:lo + FFN_CHUNK, :], preferred_element_type=F32)
    y = acc_ref[...]
    if apply_final:
        y = _rms_rows(y, fg_ref[...])
    y_ref[...] = y


def _post(o, x, w_o, ffn_gain, w_gu, w_out, final_gain, apply_final):
    b, s, d = x.shape
    m = b * s
    tm = _tile(m, 512)
    row = lambda i: (i, 0)
    y = pl.pallas_call(
        functools.partial(_post_kernel, apply_final=apply_final),
        grid=(m // tm,),
        in_specs=[
            pl.BlockSpec((tm, d), row),
            pl.BlockSpec((tm, d), row),
            _const_spec((d, d)),
            _const_spec((1, d)),
            _const_spec((d, 2 * FFN_DIM)),
            _const_spec((FFN_DIM, d)),
            _const_spec((1, d)),
        ],
        out_specs=pl.BlockSpec((tm, d), row),
        out_shape=jax.ShapeDtypeStruct((m, d), F32),
        scratch_shapes=[pltpu.VMEM((tm, d), F32)],
        compiler_params=_params(("parallel",)),
        name="post",
    )(o.reshape(m, d), x.reshape(m, d), w_o.astype(BF16), ffn_gain.reshape(1, d),
      w_gu.astype(BF16), w_out.astype(BF16), final_gain.reshape(1, d))
    return y.reshape(b, s, d)


def kernel(x, attn_norm, ffn_norm, final_norm, a_w_qkv, a_lambda_q1, a_lambda_k1, a_lambda_q2,
           a_lambda_k2, a_subln, a_w_o, b_w_a, b_q_norm, b_kv_norm, b_w_qb, b_w_kvb, b_w_o,
           c_w_qkv, c_w_o, f_w_gu, f_w_out):
    depth = attn_norm.shape[0]
    s = x.shape[1]
    cos_p, sin_p = _rope_tables(s, A_ROT)
    cos_b, sin_b = _rope_tables(s, B_ROPE)
    for i in range(depth):
        mixer, j = i % N_MIXERS, i // N_MIXERS
        if mixer == 0:
            lambda_init = 0.8 - 0.6 * math.exp(-0.3 * i)
            o = _diff_attention(x, attn_norm[i], a_w_qkv[j], a_lambda_q1[j], a_lambda_k1[j],
                                a_lambda_q2[j], a_lambda_k2[j], a_subln[j], lambda_init, cos_p, sin_p)
            w_o = a_w_o[j]
        elif mixer == 1:
            o = _latent_attention(x, attn_norm[i], b_w_a[j], b_q_norm[j], b_kv_norm[j], b_w_qb[j],
                                  b_w_kvb[j], cos_b, sin_b)
            w_o = b_w_o[j]
        else:
            o = _dilated_attention(x, attn_norm[i], c_w_qkv[j], cos_p, sin_p)
            w_o = c_w_o[j]
        x = _post(o, x, w_o, ffn_norm[i], f_w_gu[i], f_w_out[i], final_norm, i == depth - 1)
    return x
```

```python
import functools
import math

import jax
import jax.numpy as jnp
from jax import lax
from jax.experimental import pallas as pl
from jax.experimental.pallas import tpu as pltpu

F32 = jnp.float32
BF16 = jnp.bfloat16

D_MODEL = 1024
N_MIXERS = 3
ROPE_THETA = 500000.0
EPS = 1e-6
NEG_BIG = -1e30
LOG2E = 1.4426950408889634

A_HEADS = 8
A_HEAD_DIM = 64
A_V_DIM = 128
A_ROT = 16

B_HEADS = 16
B_Q_RANK = 384
B_KV_RANK = 256
B_NOPE = 64
B_ROPE = 32
B_V = 64

C_GROUPS = ((128, 1), (512, 4), (2048, 16))
C_HEADS = 16
C_HEAD_DIM = 64
C_ROT = 16
C_HALO = 64

FFN_DIM = 2816
FFN_CHUNK = 256

LANES = 128
ONES_ROWS = 16
VMEM_LIMIT_BYTES = 56 * 1024 * 1024

NT_DIMS = (((1,), (1,)), ((), ()))


def _tile(n, pref):
    t = min(n, pref)
    assert n % t == 0, (n, t)
    return t


def _params(sem):
    return pltpu.CompilerParams(dimension_semantics=sem, vmem_limit_bytes=VMEM_LIMIT_BYTES)


def _const_spec(shape):
    nd = len(shape)
    return pl.BlockSpec(shape, lambda *_: (0,) * nd, pipeline_mode=pl.Buffered(1))


def _rms_rows(x, gain):
    return x * lax.rsqrt(jnp.mean(x * x, axis=-1, keepdims=True) + EPS) * gain


def _rope_tables(seq_len, rot_dim):
    pos = jnp.arange(seq_len, dtype=F32)
    inv = ROPE_THETA ** (-jnp.arange(0, rot_dim, 2, dtype=F32) / rot_dim)
    ang = pos[:, None] * inv[None, :]
    return jnp.cos(ang), jnp.sin(ang)


def _lane_tables(cos, sin, period, offset, scale):
    s, r = cos.shape
    c = jnp.ones((s, period), F32)
    c = c.at[:, offset:offset + r].set(cos).at[:, offset + r:offset + 2 * r].set(cos)
    sa = jnp.zeros((s, period), F32).at[:, offset:offset + r].set(-sin)
    sb = jnp.zeros((s, period), F32).at[:, offset + r:offset + 2 * r].set(sin)
    rep = LANES // period
    return jnp.stack([jnp.tile(t, (1, rep)) for t in (c, sa, sb)]) * scale


def _rope(x, tab_ref, base, r):
    return (x * tab_ref[base]
            + pltpu.roll(x, LANES - r, 1) * tab_ref[base + 1]
            + pltpu.roll(x, r, 1) * tab_ref[base + 2])


def _softmax_t(chains, k_ref, vt_ref, tk):
    rows = vt_ref.shape[-2]
    nchunk = vt_ref.shape[-3]
    state = [(jnp.full((1, qm.shape[0]), NEG_BIG, F32), jnp.zeros((rows, qm.shape[0]), F32))
             for qm, _, _ in chains]
    items = [(c, n) for c in range(nchunk) for n in range(len(chains))]

    def scores(item):
        c, n = item
        qm, k_lanes, _ = chains[n]
        return lax.dot_general(k_ref[c * tk:(c + 1) * tk, k_lanes], qm, NT_DIMS,
                               preferred_element_type=F32)

    st = scores(items[0])
    for i, (c, n) in enumerate(items):
        st_next = scores(items[i + 1]) if i + 1 < len(items) else None
        m, acc = state[n]
        m_new = jnp.maximum(m, jnp.max(st, axis=0, keepdims=True))
        alpha = jnp.exp2(m - m_new)
        pt = jnp.exp2(st - m_new).astype(BF16)
        pv = jnp.dot(vt_ref[chains[n][2] + (c,)], pt, preferred_element_type=F32)
        state[n] = (m_new, acc * alpha + pv)
        st = st_next
    return [acc for _, acc in state]


def _proj_diff_kernel(x_ref, g_ref, wqk_ref, wvt_ref, tab_ref, q_ref, k_ref, vt_ref, *, tk):
    tm = x_ref.shape[0]
    h = _rms_rows(x_ref[...], g_ref[...]).astype(BF16)
    for c in range(4):
        y = jnp.dot(h, wqk_ref[:, c * 512:(c + 1) * 512], preferred_element_type=F32)
        dst, base = (q_ref, 0) if c < 2 else (k_ref, 3)
        for j in range(4):
            lo = (c % 2) * 512 + j * LANES
            dst[:, lo:lo + LANES] = _rope(y[:, j * LANES:(j + 1) * LANES], tab_ref, base,
                                          A_ROT // 2).astype(BF16)
    for half in range(2):
        vt = lax.dot_general(wvt_ref[half * 512:(half + 1) * 512, :], h, NT_DIMS,
                             preferred_element_type=F32)
        for hh in range(4):
            hv = half * 4 + hh
            for c in range(tm // tk):
                vt_ref[hv, c, 0:A_V_DIM, :] = vt[hh * A_V_DIM:(hh + 1) * A_V_DIM,
                                                 c * tk:(c + 1) * tk].astype(BF16)
                vt_ref[hv, c, A_V_DIM:A_V_DIM + ONES_ROWS, :] = jnp.ones((ONES_ROWS, tk), BF16)


def _attn_diff_kernel(lam_ref, sub_ref, q_ref, k_ref, vt_ref, o_ref, *, tk, lambda_init):
    q2 = q_ref[...]
    low = lax.broadcasted_iota(jnp.int32, q2.shape, 1) < A_HEAD_DIM
    zero = jnp.zeros_like(q2)
    chains = [(jnp.where(low, q2, zero), slice(None), ()), (jnp.where(low, zero, q2), slice(None), ())]
    outs = [acc[0:A_V_DIM] * (1.0 / acc[A_V_DIM:A_V_DIM + 1])
            for acc in _softmax_t(chains, k_ref, vt_ref, tk)]
    l1 = jnp.sum(lam_ref[0:1, :] * lam_ref[1:2, :], axis=-1, keepdims=True)
    l2 = jnp.sum(lam_ref[2:3, :] * lam_ref[3:4, :], axis=-1, keepdims=True)
    lam = jnp.exp(l1) - jnp.exp(l2) + lambda_init
    o = (outs[0] - lam * outs[1]).T
    o_ref[...] = (_rms_rows(o, sub_ref[...]) * (1.0 - lambda_init)).astype(o_ref.dtype)


def _diff_attention(x, gain, w_qkv, lam_q1, lam_k1, lam_q2, lam_k2, subln, lambda_init, cos, sin):
    b, s, d = x.shape
    tm = _tile(s, 512)
    tq = _tile(s, 512)
    tk = _tile(s, 512)
    tm = max(tm, tk)
    nck = s // tk
    qk_w = 2 * A_HEADS * A_HEAD_DIM
    scale = A_HEAD_DIM ** -0.5 * LOG2E
    tab = jnp.concatenate([_lane_tables(cos, sin, A_HEAD_DIM, 0, scale),
                           _lane_tables(cos, sin, A_HEAD_DIM, 0, 1.0)])
    wqk = w_qkv[:, :2 * qk_w].astype(BF16)
    wvt = w_qkv[:, 2 * qk_w:].T.astype(BF16)
    vrows = A_V_DIM + ONES_ROWS

    q, k, vt = pl.pallas_call(
        functools.partial(_proj_diff_kernel, tk=tk),
        grid=(b, s // tm),
        in_specs=[
            pl.BlockSpec((None, tm, d), lambda bi, i: (bi, i, 0)),
            _const_spec((1, d)),
            _const_spec((d, 2 * qk_w)),
            _const_spec((A_HEADS * A_V_DIM, d)),
            pl.BlockSpec((6, tm, LANES), lambda bi, i: (0, i, 0)),
        ],
        out_specs=[
            pl.BlockSpec((None, tm, qk_w), lambda bi, i: (bi, i, 0)),
            pl.BlockSpec((None, tm, qk_w), lambda bi, i: (bi, i, 0)),
            pl.BlockSpec((None, A_HEADS, tm // tk, vrows, tk), lambda bi, i: (bi, 0, i, 0, 0)),
        ],
        out_shape=[
            jax.ShapeDtypeStruct((b, s, qk_w), BF16),
            jax.ShapeDtypeStruct((b, s, qk_w), BF16),
            jax.ShapeDtypeStruct((b, A_HEADS, nck, vrows, tk), BF16),
        ],
        compiler_params=_params(("parallel", "parallel")),
        name="proj_diff",
    )(x, gain.reshape(1, d), wqk, wvt, tab)

    lam = jnp.stack([lam_q1, lam_k1, lam_q2, lam_k2]).astype(F32)
    return pl.pallas_call(
        functools.partial(_attn_diff_kernel, tk=tk, lambda_init=lambda_init),
        grid=(b, A_HEADS, s // tq),
        in_specs=[
            _const_spec((4, A_HEAD_DIM)),
            _const_spec((1, A_V_DIM)),
            pl.BlockSpec((None, tq, LANES), lambda bi, h, i: (bi, i, h)),
            pl.BlockSpec((None, s, LANES), lambda bi, h, i: (bi, 0, h)),
            pl.BlockSpec((None, None, nck, vrows, tk), lambda bi, h, i: (bi, h, 0, 0, 0)),
        ],
        out_specs=pl.BlockSpec((None, tq, A_V_DIM), lambda bi, h, i: (bi, i, h)),
        out_shape=jax.ShapeDtypeStruct((b, s, A_HEADS * A_V_DIM), BF16),
        compiler_params=_params(("parallel", "parallel", "arbitrary")),
        name="attn_diff",
    )(lam, subln.reshape(1, A_V_DIM).astype(F32), q, k, vt)


def _proj_mla_kernel(x_ref, g_ref, waq_ref, wakv_ref, war_ref, qn_ref, kvn_ref, wqb_ref, wkb_ref,
                     wvt_ref, tab_ref, q_ref, k_ref, vt_ref, *, tk):
    tm = x_ref.shape[0]
    r = B_ROPE // 2
    h = _rms_rows(x_ref[...], g_ref[...]).astype(BF16)
    q_lat = _rms_rows(jnp.dot(h, waq_ref[...], preferred_element_type=F32), qn_ref[...]).astype(BF16)
    kv_lat = _rms_rows(jnp.dot(h, wakv_ref[...], preferred_element_type=F32), kvn_ref[...]).astype(BF16)
    k_rope = _rope(jnp.dot(h, war_ref[...], preferred_element_type=F32), tab_ref, 3, r)
    for c in range(4):
        yq = jnp.dot(q_lat, wqb_ref[:, c * 512:(c + 1) * 512], preferred_element_type=F32)
        yk = jnp.dot(kv_lat, wkb_ref[:, c * 512:(c + 1) * 512], preferred_element_type=F32)
        for j in range(4):
            lo = c * 512 + j * LANES
            q_ref[:, lo:lo + LANES] = _rope(yq[:, j * LANES:(j + 1) * LANES], tab_ref, 0, r).astype(BF16)
            k_ref[:, lo:lo + LANES] = (yk[:, j * LANES:(j + 1) * LANES] + k_rope).astype(BF16)
    for half in range(2):
        vt = lax.dot_general(wvt_ref[half * 512:(half + 1) * 512, :], kv_lat, NT_DIMS,
                             preferred_element_type=F32)
        for hh in range(8):
            hv = half * 8 + hh
            for c in range(tm // tk):
                vt_ref[hv, c, 0:B_V, :] = vt[hh * B_V:(hh + 1) * B_V, c * tk:(c + 1) * tk].astype(BF16)
                vt_ref[hv, c, B_V:B_V + ONES_ROWS, :] = jnp.ones((ONES_ROWS, tk), BF16)


def _attn_mla_kernel(q_ref, k_ref, vt_ref, o_ref, *, tk):
    chains = [(q_ref[:, e * LANES:(e + 1) * LANES], slice(e * LANES, (e + 1) * LANES), (e,))
              for e in range(2)]
    outs = [acc[0:B_V] * (1.0 / acc[B_V:B_V + 1]) for acc in _softmax_t(chains, k_ref, vt_ref, tk)]
    o_ref[...] = jnp.concatenate(outs, axis=0).T.astype(o_ref.dtype)


def _latent_attention(x, gain, w_a, q_norm, kv_norm, w_qb, w_kvb, cos, sin):
    b, s, d = x.shape
    tq = _tile(s, 512)
    tk = _tile(s, 512)
    tm = tk
    nck = s // tk
    hw = B_HEADS * LANES
    scale = (B_NOPE + B_ROPE) ** -0.5 * LOG2E
    tab = jnp.concatenate([_lane_tables(cos, sin, LANES, B_NOPE, scale),
                           _lane_tables(cos, sin, LANES, B_NOPE, 1.0)])
    waq = w_a[:, :B_Q_RANK].astype(BF16)
    wakv = w_a[:, B_Q_RANK:B_Q_RANK + B_KV_RANK].astype(BF16)
    war = jnp.zeros((d, LANES), F32).at[:, B_NOPE:B_NOPE + B_ROPE].set(
        w_a[:, B_Q_RANK + B_KV_RANK:]).astype(BF16)
    pad = LANES - (B_NOPE + B_ROPE)
    wqb = jnp.pad(w_qb.reshape(B_Q_RANK, B_HEADS, B_NOPE + B_ROPE),
                  ((0, 0), (0, 0), (0, pad))).reshape(B_Q_RANK, hw).astype(BF16)
    wkv3 = w_kvb.reshape(B_KV_RANK, B_HEADS, B_NOPE + B_V)
    wkb = jnp.pad(wkv3[:, :, :B_NOPE], ((0, 0), (0, 0), (0, LANES - B_NOPE))
                  ).reshape(B_KV_RANK, hw).astype(BF16)
    wvt = wkv3[:, :, B_NOPE:].reshape(B_KV_RANK, B_HEADS * B_V).T.astype(BF16)
    vrows = B_V + ONES_ROWS

    q, k, vt = pl.pallas_call(
        functools.partial(_proj_mla_kernel, tk=tk),
        grid=(b, s // tm),
        in_specs=[
            pl.BlockSpec((None, tm, d), lambda bi, i: (bi, i, 0)),
            _const_spec((1, d)),
            _const_spec((d, B_Q_RANK)),
            _const_spec((d, B_KV_RANK)),
            _const_spec((d, LANES)),
            _const_spec((1, B_Q_RANK)),
            _const_spec((1, B_KV_RANK)),
            _const_spec((B_Q_RANK, hw)),
            _const_spec((B_KV_RANK, hw)),
            _const_spec((B_HEADS * B_V, B_KV_RANK)),
            pl.BlockSpec((6, tm, LANES), lambda bi, i: (0, i, 0)),
        ],
        out_specs=[
            pl.BlockSpec((None, tm, hw), lambda bi, i: (bi, i, 0)),
            pl.BlockSpec((None, tm, hw), lambda bi, i: (bi, i, 0)),
            pl.BlockSpec((None, B_HEADS, tm // tk, vrows, tk), lambda bi, i: (bi, 0, i, 0, 0)),
        ],
        out_shape=[
            jax.ShapeDtypeStruct((b, s, hw), BF16),
            jax.ShapeDtypeStruct((b, s, hw), BF16),
            jax.ShapeDtypeStruct((b, B_HEADS, nck, vrows, tk), BF16),
        ],
        compiler_params=_params(("parallel", "parallel")),
        name="proj_mla",
    )(x, gain.reshape(1, d), waq, wakv, war, q_norm.reshape(1, -1), kv_norm.reshape(1, -1),
      wqb, wkb, wvt, tab)

    return pl.pallas_call(
        functools.partial(_attn_mla_kernel, tk=tk),
        grid=(b, B_HEADS // 2, s // tq),
        in_specs=[
            pl.BlockSpec((None, tq, 2 * LANES), lambda bi, h, i: (bi, i, h)),
            pl.BlockSpec((None, s, 2 * LANES), lambda bi, h, i: (bi, 0, h)),
            pl.BlockSpec((None, 2, nck, vrows, tk), lambda bi, h, i: (bi, h, 0, 0, 0)),
        ],
        out_specs=pl.BlockSpec((None, tq, 2 * B_V), lambda bi, h, i: (bi, i, h)),
        out_shape=jax.ShapeDtypeStruct((b, s, B_HEADS * B_V), BF16),
        compiler_params=_params(("parallel", "parallel", "arbitrary")),
        name="attn_mla",
    )(q, k, vt)


def _proj_dil_kernel(x_ref, g_ref, w_ref, tab_ref, q_ref, k_ref, v_ref):
    h = _rms_rows(x_ref[...], g_ref[...]).astype(BF16)
    width = C_HEADS * C_HEAD_DIM
    for c in range(3 * width // 512):
        y = jnp.dot(h, w_ref[:, c * 512:(c + 1) * 512], preferred_element_type=F32)
        which = c // 2
        dst = (q_ref, k_ref, v_ref)[which]
        for j in range(4):
            lo = (c % 2) * 512 + j * LANES
            yj = y[:, j * LANES:(j + 1) * LANES]
            if which < 2:
                yj = _rope(yj, tab_ref, 3 * which, C_ROT // 2)
            dst[:, lo:lo + LANES] = yj.astype(BF16)


def _attn_dil_kernel(q_ref, kp_ref, kc_ref, kn_ref, vp_ref, vc_ref, vn_ref, o_ref, lse_ref, *, seq):
    tq = q_ref.shape[0]
    w = tq + 2 * C_HALO
    q0 = pl.program_id(2) * tq
    qpos = q0 + lax.broadcasted_iota(jnp.int32, (tq, w), 0)
    kpos = q0 - C_HALO + lax.broadcasted_iota(jnp.int32, (tq, w), 1)
    mask = (jnp.abs(qpos - kpos) <= C_HALO) & (kpos >= 0) & (kpos < seq)
    low = lax.broadcasted_iota(jnp.int32, (tq, LANES), 1) < C_HEAD_DIM
    head_lane = lax.broadcasted_iota(jnp.int32, (tq, LANES), 1)
    lse_all = jnp.zeros((tq, LANES), F32)
    for j in range(C_HEADS // 2):
        lanes = slice(j * LANES, (j + 1) * LANES)
        q2 = q_ref[:, lanes]
        zero = jnp.zeros_like(q2)
        k2 = jnp.concatenate([kp_ref[:, lanes], kc_ref[:, lanes], kn_ref[:, lanes]], axis=0)
        v2 = jnp.concatenate([vp_ref[:, lanes], vc_ref[:, lanes], vn_ref[:, lanes]], axis=0)
        halves = []
        for e in range(2):
            qm = jnp.where(low, q2, zero) if e == 0 else jnp.where(low, zero, q2)
            sc = lax.dot_general(qm, k2, NT_DIMS, preferred_element_type=F32)
            sc = jnp.where(mask, sc, NEG_BIG)
            m = jnp.max(sc, axis=-1, keepdims=True)
            p = jnp.exp2(sc - m)
            l = jnp.sum(p, axis=-1, keepdims=True)
            pv = jnp.dot(p.astype(BF16), v2, preferred_element_type=F32)
            halves.append(pv * (1.0 / l))
            lse_all = jnp.where(head_lane == 2 * j + e, m + jnp.log2(l), lse_all)
        o_ref[:, lanes] = jnp.where(low, halves[0], halves[1]).astype(o_ref.dtype)
    lse_ref[...] = lse_all


def _combine_dil_kernel(o1_ref, o2_ref, o3_ref, l1_ref, l2_ref, l3_ref, o_ref):
    ls = [l1_ref[...], l2_ref[...], l3_ref[...]]
    m = jnp.maximum(jnp.maximum(ls[0], ls[1]), ls[2])
    es = [jnp.exp2(l - m) for l in ls]
    inv = 1.0 / (es[0] + es[1] + es[2])
    width = C_HEADS * C_HEAD_DIM
    expand = (lax.broadcasted_iota(jnp.int32, (LANES, width), 1) // C_HEAD_DIM
              == lax.broadcasted_iota(jnp.int32, (LANES, width), 0)).astype(BF16)
    acc = jnp.zeros(o_ref.shape, F32)
    for e, oref in zip(es, (o1_ref, o2_ref, o3_ref)):
        wt = e * inv
        hi = wt.astype(BF16)
        lo = (wt - hi.astype(F32)).astype(BF16)
        wide = (jnp.dot(hi, expand, preferred_element_type=F32)
                + jnp.dot(lo, expand, preferred_element_type=F32))
        acc = acc + wide * oref[...].astype(F32)
    o_ref[...] = acc.astype(o_ref.dtype)


def _dilated_attention(x, gain, w_qkv, cos, sin):
    b, s, d = x.shape
    width = C_HEADS * C_HEAD_DIM
    scale = C_HEAD_DIM ** -0.5 * LOG2E
    tab = jnp.concatenate([_lane_tables(cos, sin, C_HEAD_DIM, 0, scale),
                           _lane_tables(cos, sin, C_HEAD_DIM, 0, 1.0)])
    w_bf = w_qkv.astype(BF16)
    outs, lses = [], []
    for g, (window, dil) in enumerate(C_GROUPS):
        assert (window // 2) // dil == C_HALO
        seq = s // dil
        assert seq % C_HALO == 0
        tl = _tile(seq, 512)
        xg = x.reshape(b, seq, dil * d)
        tabg = tab.reshape(6, seq, dil * LANES)
        q, k, v = pl.pallas_call(
            _proj_dil_kernel,
            grid=(b, dil, seq // tl),
            in_specs=[
                pl.BlockSpec((None, tl, d), lambda bi, p, i: (bi, i, p)),
                _const_spec((1, d)),
                _const_spec((d, 3 * width)),
                pl.BlockSpec((6, tl, LANES), lambda bi, p, i: (0, i, p)),
            ],
            out_specs=[pl.BlockSpec((None, None, tl, width), lambda bi, p, i: (bi, p, i, 0))] * 3,
            out_shape=[jax.ShapeDtypeStruct((b, dil, seq, width), BF16)] * 3,
            compiler_params=_params(("parallel", "parallel", "parallel")),
            name=f"proj_dil{g}",
        )(xg, gain.reshape(1, d), w_bf[:, g * 3 * width:(g + 1) * 3 * width], tabg)

        tq = _tile(seq, 256)
        r = tq // C_HALO
        nhalo = seq // C_HALO
        cur = pl.BlockSpec((None, None, tq, width), lambda bi, p, i: (bi, p, i, 0))
        prev = pl.BlockSpec((None, None, C_HALO, width),
                            lambda bi, p, i: (bi, p, jnp.maximum(i * r - 1, 0), 0))
        nxt = pl.BlockSpec((None, None, C_HALO, width),
                           lambda bi, p, i: (bi, p, jnp.minimum((i + 1) * r, nhalo - 1), 0))
        o_g, lse_g = pl.pallas_call(
            functools.partial(_attn_dil_kernel, seq=seq),
            grid=(b, dil, seq // tq),
            in_specs=[cur, prev, cur, nxt, prev, cur, nxt],
            out_specs=[cur, pl.BlockSpec((None, None, tq, LANES), lambda bi, p, i: (bi, p, i, 0))],
            out_shape=[jax.ShapeDtypeStruct((b, dil, seq, width), BF16),
                       jax.ShapeDtypeStruct((b, dil, seq, LANES), F32)],
            compiler_params=_params(("parallel", "parallel", "parallel")),
            name=f"attn_dil{g}",
        )(q, k, k, k, v, v, v)
        outs.append(o_g)
        lses.append(lse_g)

    dmax = C_GROUPS[-1][1]
    rows = s // dmax
    ins, specs = [], []
    for cols, arrs in ((width, outs), (LANES, lses)):
        for arr, (_, dil) in zip(arrs, C_GROUPS):
            sub = dmax // dil
            ins.append(arr.reshape(b, dil, rows, sub * cols))
            specs.append(pl.BlockSpec((None, None, rows, cols),
                                      lambda bi, p, dil=dil: (bi, p % dil, 0, p // dil)))
    o = pl.pallas_call(
        _combine_dil_kernel,
        grid=(b, dmax),
        in_specs=specs,
        out_specs=pl.BlockSpec((None, rows, width), lambda bi, p: (bi, 0, p)),
        out_shape=jax.ShapeDtypeStruct((b, rows, dmax * width), BF16),
        compiler_params=_params(("parallel", "parallel")),
        name="combine_dil",
    )(*ins)
    return o.reshape(b, s, width)


def _post_kernel(o_ref, x_ref, wo_ref, g_ref, wgu_ref, wout_ref, fg_ref, y_ref, acc_ref, *,
                 apply_final):
    x1 = x_ref[...] + jnp.dot(o_ref[...], wo_ref[...], preferred_element_type=F32)
    h = _rms_rows(x1, g_ref[...]).astype(BF16)
    acc_ref[...] = x1
    for c in range(FFN_DIM // FFN_CHUNK):
        lo = c * FFN_CHUNK
        gate = jnp.dot(h, wgu_ref[:, lo:lo + FFN_CHUNK], preferred_element_type=F32)
        up = jnp.dot(h, wgu_ref[:, FFN_DIM + lo:FFN_DIM + lo + FFN_CHUNK], preferred_element_type=F32)
        act = (gate * jax.nn.sigmoid(gate) * up).astype(BF16)
        acc_ref[...] += jnp.dot(act, wout_ref[lo:lo + FFN_CHUNK, :], preferred_element_type=F32)
    y = acc_ref[...]
    if apply_final:
        y = _rms_rows(y, fg_ref[...])
    y_ref[...] = y


def _post(o, x, w_o, ffn_gain, w_gu, w_out, final_gain, apply_final):
    b, s, d = x.shape
    m = b * s
    tm = _tile(m, 512)
    row = lambda i: (i, 0)
    y = pl.pallas_call(
        functools.partial(_post_kernel, apply_final=apply_final),
        grid=(m // tm,),
        in_specs=[
            pl.BlockSpec((tm, d), row),
            pl.BlockSpec((tm, d), row),
            _const_spec((d, d)),
            _const_spec((1, d)),
            _const_spec((d, 2 * FFN_DIM)),
            _const_spec((FFN_DIM, d)),
            _const_spec((1, d)),
        ],
        out_specs=pl.BlockSpec((tm, d), row),
        out_shape=jax.ShapeDtypeStruct((m, d), F32),
        scratch_shapes=[pltpu.VMEM((tm, d), F32)],
        compiler_params=_params(("parallel",)),
        name="post",
    )(o.reshape(m, d), x.reshape(m, d), w_o.astype(BF16), ffn_gain.reshape(1, d),
      w_gu.astype(BF16), w_out.astype(BF16), final_gain.reshape(1, d))
    return y.reshape(b, s, d)


def kernel(x, attn_norm, ffn_norm, final_norm, a_w_qkv, a_lambda_q1, a_lambda_k1, a_lambda_q2,
           a_lambda_k2, a_subln, a_w_o, b_w_a, b_q_norm, b_kv_norm, b_w_qb, b_w_kvb, b_w_o,
           c_w_qkv, c_w_o, f_w_gu, f_w_out):
    depth = attn_norm.shape[0]
    s = x.shape[1]
    cos_p, sin_p = _rope_tables(s, A_ROT)
    cos_b, sin_b = _rope_tables(s, B_ROPE)
    for i in range(depth):
        mixer, j = i % N_MIXERS, i // N_MIXERS
        if mixer == 0:
            lambda_init = 0.8 - 0.6 * math.exp(-0.3 * i)
            o = _diff_attention(x, attn_norm[i], a_w_qkv[j], a_lambda_q1[j], a_lambda_k1[j],
                                a_lambda_q2[j], a_lambda_k2[j], a_subln[j], lambda_init, cos_p, sin_p)
            w_o = a_w_o[j]
        elif mixer == 1:
            o = _latent_attention(x, attn_norm[i], b_w_a[j], b_q_norm[j], b_kv_norm[j], b_w_qb[j],
                                  b_w_kvb[j], cos_b, sin_b)
            w_o = b_w_o[j]
        else:
            o = _dilated_attention(x, attn_norm[i], c_w_qkv[j], cos_p, sin_p)
            w_o = c_w_o[j]
        x = _post(o, x, w_o, ffn_norm[i], f_w_gu[i], f_w_out[i], final_norm, i == depth - 1)
    return x
```

```python
import functools
import math

import jax
import jax.numpy as jnp
from jax import lax
from jax.experimental import pallas as pl
from jax.experimental.pallas import tpu as pltpu

F32 = jnp.float32
BF16 = jnp.bfloat16

D_MODEL = 1024
N_MIXERS = 3
ROPE_THETA = 500000.0
EPS = 1e-6
NEG_BIG = -1e30
LOG2E = 1.4426950408889634

A_HEADS = 8
A_HEAD_DIM = 64
A_V_DIM = 128
A_ROT = 16

B_HEADS = 16
B_Q_RANK = 384
B_KV_RANK = 256
B_NOPE = 64
B_ROPE = 32
B_V = 64

C_GROUPS = ((128, 1), (512, 4), (2048, 16))
C_HEADS = 16
C_HEAD_DIM = 64
C_ROT = 16
C_HALO = 64

FFN_DIM = 2816
FFN_CHUNK = 256

LANES = 128
ONES_ROWS = 16
VMEM_LIMIT_BYTES = 56 * 1024 * 1024

NT_DIMS = (((1,), (1,)), ((), ()))
BOUND_MARGIN = 1.01
MIN_DENOMINATOR = 2.0 ** -60


def _tile(n, pref):
    t = min(n, pref)
    assert n % t == 0, (n, t)
    return t


def _params(sem):
    return pltpu.CompilerParams(dimension_semantics=sem, vmem_limit_bytes=VMEM_LIMIT_BYTES)


def _const_spec(shape):
    nd = len(shape)
    return pl.BlockSpec(shape, lambda *_: (0,) * nd, pipeline_mode=pl.Buffered(1))


def _rms_rows(x, gain):
    return x * lax.rsqrt(jnp.mean(x * x, axis=-1, keepdims=True) + EPS) * gain


def _rope_tables(seq_len, rot_dim):
    pos = jnp.arange(seq_len, dtype=F32)
    inv = ROPE_THETA ** (-jnp.arange(0, rot_dim, 2, dtype=F32) / rot_dim)
    ang = pos[:, None] * inv[None, :]
    return jnp.cos(ang), jnp.sin(ang)


def _lane_tables(cos, sin, period, offset, scale):
    s, r = cos.shape
    c = jnp.ones((s, period), F32)
    c = c.at[:, offset:offset + r].set(cos).at[:, offset + r:offset + 2 * r].set(cos)
    sa = jnp.zeros((s, period), F32).at[:, offset:offset + r].set(-sin)
    sb = jnp.zeros((s, period), F32).at[:, offset + r:offset + 2 * r].set(sin)
    rep = LANES // period
    return jnp.stack([jnp.tile(t, (1, rep)) for t in (c, sa, sb)]) * scale


def _rope(x, tab_ref, base, r):
    return (x * tab_ref[base]
            + pltpu.roll(x, LANES - r, 1) * tab_ref[base + 1]
            + pltpu.roll(x, r, 1) * tab_ref[base + 2])


def _softmax_t(chains, k_ref, vt_ref, tk, shifts=None):
    rows = vt_ref.shape[-2]
    nchunk = vt_ref.shape[-3]
    state = [(jnp.full((1, qm.shape[0]), NEG_BIG, F32), jnp.zeros((rows, qm.shape[0]), F32))
             for qm, _, _ in chains]
    items = [(c, n) for c in range(nchunk) for n in range(len(chains))]

    def scores(item):
        c, n = item
        qm, k_lanes, _ = chains[n]
        return lax.dot_general(k_ref[c * tk:(c + 1) * tk, k_lanes], qm, NT_DIMS,
                               preferred_element_type=F32)

    st = scores(items[0])
    for i, (c, n) in enumerate(items):
        st_next = scores(items[i + 1]) if i + 1 < len(items) else None
        m, acc = state[n]
        vt = vt_ref[chains[n][2] + (c,)]
        if shifts is None:
            m_new = jnp.maximum(m, jnp.max(st, axis=0, keepdims=True))
            pt = jnp.exp2(st - m_new).astype(BF16)
            acc = acc * jnp.exp2(m - m_new)
        else:
            m_new = m
            pt = jnp.exp2(st - shifts[n]).astype(BF16)
        state[n] = (m_new, acc + jnp.dot(vt, pt, preferred_element_type=F32))
        st = st_next
    return [acc for _, acc in state]


def _key_norm_bound(k_ref, k_lanes, sel, width):
    kk = k_ref[:, k_lanes]
    kn2 = lax.dot_general(sel, kk * kk, NT_DIMS, preferred_element_type=F32)
    return jnp.broadcast_to(jnp.sqrt(jnp.max(kn2, axis=-1, keepdims=True)), (8, width))


def _score_bound(qm, kmax):
    ones = jnp.ones((8, qm.shape[1]), BF16)
    qn2 = lax.dot_general(ones, qm * qm, NT_DIMS, preferred_element_type=F32)
    return jnp.sqrt(qn2[0:1]) * kmax[0:1] * BOUND_MARGIN


def _denominators_ok(accs, row):
    lmin = accs[0][row:row + 1]
    for acc in accs[1:]:
        lmin = jnp.minimum(lmin, acc[row:row + 1])
    return jnp.min(lmin) >= MIN_DENOMINATOR


def _proj_diff_kernel(x_ref, g_ref, wqk_ref, wvt_ref, tab_ref, q_ref, k_ref, vt_ref, *, tk):
    tm = x_ref.shape[0]
    h = _rms_rows(x_ref[...], g_ref[...]).astype(BF16)
    for c in range(4):
        y = jnp.dot(h, wqk_ref[:, c * 512:(c + 1) * 512], preferred_element_type=F32)
        dst, base = (q_ref, 0) if c < 2 else (k_ref, 3)
        for j in range(4):
            lo = (c % 2) * 512 + j * LANES
            dst[:, lo:lo + LANES] = _rope(y[:, j * LANES:(j + 1) * LANES], tab_ref, base,
                                          A_ROT // 2).astype(BF16)
    for half in range(2):
        vt = lax.dot_general(wvt_ref[half * 512:(half + 1) * 512, :], h, NT_DIMS,
                             preferred_element_type=F32)
        for hh in range(4):
            hv = half * 4 + hh
            for c in range(tm // tk):
                vt_ref[hv, c, 0:A_V_DIM, :] = vt[hh * A_V_DIM:(hh + 1) * A_V_DIM,
                                                 c * tk:(c + 1) * tk].astype(BF16)
                vt_ref[hv, c, A_V_DIM:A_V_DIM + ONES_ROWS, :] = jnp.ones((ONES_ROWS, tk), BF16)


def _attn_diff_kernel(lam_ref, sub_ref, q_ref, k_ref, vt_ref, o_ref, kmax_ref, *, tk, lambda_init):
    q2 = q_ref[...]
    low = lax.broadcasted_iota(jnp.int32, q2.shape, 1) < A_HEAD_DIM
    zero = jnp.zeros_like(q2)
    chains = [(jnp.where(low, q2, zero), slice(None), ()), (jnp.where(low, zero, q2), slice(None), ())]

    @pl.when(pl.program_id(2) == 0)
    def _():
        sel_low = lax.broadcasted_iota(jnp.int32, (8, LANES), 1) < A_HEAD_DIM
        for n in range(2):
            sel = (sel_low if n == 0 else jnp.logical_not(sel_low)).astype(BF16)
            kmax_ref[n] = _key_norm_bound(k_ref, slice(None), sel, kmax_ref.shape[-1])

    l1 = jnp.sum(lam_ref[0:1, :] * lam_ref[1:2, :], axis=-1, keepdims=True)
    l2 = jnp.sum(lam_ref[2:3, :] * lam_ref[3:4, :], axis=-1, keepdims=True)
    lam = jnp.exp(l1) - jnp.exp(l2) + lambda_init

    def finish(accs):
        outs = [acc[0:A_V_DIM] * (1.0 / acc[A_V_DIM:A_V_DIM + 1]) for acc in accs]
        o = (outs[0] - lam * outs[1]).T
        o_ref[...] = (_rms_rows(o, sub_ref[...]) * (1.0 - lambda_init)).astype(o_ref.dtype)

    shifts = [_score_bound(chains[n][0], kmax_ref[n]) for n in range(2)]
    accs = _softmax_t(chains, k_ref, vt_ref, tk, shifts)
    finish(accs)

    @pl.when(jnp.logical_not(_denominators_ok(accs, A_V_DIM)))
    def _():
        finish(_softmax_t(chains, k_ref, vt_ref, tk))


def _diff_attention(x, gain, w_qkv, lam_q1, lam_k1, lam_q2, lam_k2, subln, lambda_init, cos, sin):
    b, s, d = x.shape
    tm = _tile(s, 512)
    tq = _tile(s, 512)
    tk = _tile(s, 512)
    tm = max(tm, tk)
    nck = s // tk
    qk_w = 2 * A_HEADS * A_HEAD_DIM
    scale = A_HEAD_DIM ** -0.5 * LOG2E
    tab = jnp.concatenate([_lane_tables(cos, sin, A_HEAD_DIM, 0, scale),
                           _lane_tables(cos, sin, A_HEAD_DIM, 0, 1.0)])
    wqk = w_qkv[:, :2 * qk_w].astype(BF16)
    wvt = w_qkv[:, 2 * qk_w:].T.astype(BF16)
    vrows = A_V_DIM + ONES_ROWS

    q, k, vt = pl.pallas_call(
        functools.partial(_proj_diff_kernel, tk=tk),
        grid=(b, s // tm),
        in_specs=[
            pl.BlockSpec((None, tm, d), lambda bi, i: (bi, i, 0)),
            _const_spec((1, d)),
            _const_spec((d, 2 * qk_w)),
            _const_spec((A_HEADS * A_V_DIM, d)),
            pl.BlockSpec((6, tm, LANES), lambda bi, i: (0, i, 0)),
        ],
        out_specs=[
            pl.BlockSpec((None, tm, qk_w), lambda bi, i: (bi, i, 0)),
            pl.BlockSpec((None, tm, qk_w), lambda bi, i: (bi, i, 0)),
            pl.BlockSpec((None, A_HEADS, tm // tk, vrows, tk), lambda bi, i: (bi, 0, i, 0, 0)),
        ],
        out_shape=[
            jax.ShapeDtypeStruct((b, s, qk_w), BF16),
            jax.ShapeDtypeStruct((b, s, qk_w), BF16),
            jax.ShapeDtypeStruct((b, A_HEADS, nck, vrows, tk), BF16),
        ],
        compiler_params=_params(("parallel", "parallel")),
        name="proj_diff",
    )(x, gain.reshape(1, d), wqk, wvt, tab)

    lam = jnp.stack([lam_q1, lam_k1, lam_q2, lam_k2]).astype(F32)
    return pl.pallas_call(
        functools.partial(_attn_diff_kernel, tk=tk, lambda_init=lambda_init),
        grid=(b, A_HEADS, s // tq),
        in_specs=[
            _const_spec((4, A_HEAD_DIM)),
            _const_spec((1, A_V_DIM)),
            pl.BlockSpec((None, tq, LANES), lambda bi, h, i: (bi, i, h)),
            pl.BlockSpec((None, s, LANES), lambda bi, h, i: (bi, 0, h)),
            pl.BlockSpec((None, None, nck, vrows, tk), lambda bi, h, i: (bi, h, 0, 0, 0)),
        ],
        out_specs=pl.BlockSpec((None, tq, A_V_DIM), lambda bi, h, i: (bi, i, h)),
        out_shape=jax.ShapeDtypeStruct((b, s, A_HEADS * A_V_DIM), BF16),
        scratch_shapes=[pltpu.VMEM((2, 8, tq), F32)],
        compiler_params=_params(("parallel", "parallel", "arbitrary")),
        name="attn_diff",
    )(lam, subln.reshape(1, A_V_DIM).astype(F32), q, k, vt)


def _proj_mla_kernel(x_ref, g_ref, waq_ref, wakv_ref, war_ref, qn_ref, kvn_ref, wqb_ref, wkb_ref,
                     wvt_ref, tab_ref, q_ref, k_ref, vt_ref, *, tk):
    tm = x_ref.shape[0]
    r = B_ROPE // 2
    h = _rms_rows(x_ref[...], g_ref[...]).astype(BF16)
    q_lat = _rms_rows(jnp.dot(h, waq_ref[...], preferred_element_type=F32), qn_ref[...]).astype(BF16)
    kv_lat = _rms_rows(jnp.dot(h, wakv_ref[...], preferred_element_type=F32), kvn_ref[...]).astype(BF16)
    k_rope = _rope(jnp.dot(h, war_ref[...], preferred_element_type=F32), tab_ref, 3, r)
    for c in range(4):
        yq = jnp.dot(q_lat, wqb_ref[:, c * 512:(c + 1) * 512], preferred_element_type=F32)
        yk = jnp.dot(kv_lat, wkb_ref[:, c * 512:(c + 1) * 512], preferred_element_type=F32)
        for j in range(4):
            lo = c * 512 + j * LANES
            q_ref[:, lo:lo + LANES] = _rope(yq[:, j * LANES:(j + 1) * LANES], tab_ref, 0, r).astype(BF16)
            k_ref[:, lo:lo + LANES] = (yk[:, j * LANES:(j + 1) * LANES] + k_rope).astype(BF16)
    for half in range(2):
        vt = lax.dot_general(wvt_ref[half * 512:(half + 1) * 512, :], kv_lat, NT_DIMS,
                             preferred_element_type=F32)
        for hh in range(8):
            hv = half * 8 + hh
            for c in range(tm // tk):
                vt_ref[hv, c, 0:B_V, :] = vt[hh * B_V:(hh + 1) * B_V, c * tk:(c + 1) * tk].astype(BF16)
                vt_ref[hv, c, B_V:B_V + ONES_ROWS, :] = jnp.ones((ONES_ROWS, tk), BF16)


def _attn_mla_kernel(q_ref, k_ref, vt_ref, o_ref, kmax_ref, *, tk):
    chains = [(q_ref[:, e * LANES:(e + 1) * LANES], slice(e * LANES, (e + 1) * LANES), (e,))
              for e in range(2)]

    @pl.when(pl.program_id(2) == 0)
    def _():
        sel = jnp.ones((8, LANES), BF16)
        for n in range(2):
            kmax_ref[n] = _key_norm_bound(k_ref, chains[n][1], sel, kmax_ref.shape[-1])

    def finish(accs):
        outs = [acc[0:B_V] * (1.0 / acc[B_V:B_V + 1]) for acc in accs]
        o_ref[...] = jnp.concatenate(outs, axis=0).T.astype(o_ref.dtype)

    shifts = [_score_bound(chains[n][0], kmax_ref[n]) for n in range(2)]
    accs = _softmax_t(chains, k_ref, vt_ref, tk, shifts)
    finish(accs)

    @pl.when(jnp.logical_not(_denominators_ok(accs, B_V)))
    def _():
        finish(_softmax_t(chains, k_ref, vt_ref, tk))


def _latent_attention(x, gain, w_a, q_norm, kv_norm, w_qb, w_kvb, cos, sin):
    b, s, d = x.shape
    tq = _tile(s, 512)
    tk = _tile(s, 512)
    tm = tk
    nck = s // tk
    hw = B_HEADS * LANES
    scale = (B_NOPE + B_ROPE) ** -0.5 * LOG2E
    tab = jnp.concatenate([_lane_tables(cos, sin, LANES, B_NOPE, scale),
                           _lane_tables(cos, sin, LANES, B_NOPE, 1.0)])
    waq = w_a[:, :B_Q_RANK].astype(BF16)
    wakv = w_a[:, B_Q_RANK:B_Q_RANK + B_KV_RANK].astype(BF16)
    war = jnp.zeros((d, LANES), F32).at[:, B_NOPE:B_NOPE + B_ROPE].set(
        w_a[:, B_Q_RANK + B_KV_RANK:]).astype(BF16)
    pad = LANES - (B_NOPE + B_ROPE)
    wqb = jnp.pad(w_qb.reshape(B_Q_RANK, B_HEADS, B_NOPE + B_ROPE),
                  ((0, 0), (0, 0), (0, pad))).reshape(B_Q_RANK, hw).astype(BF16)
    wkv3 = w_kvb.reshape(B_KV_RANK, B_HEADS, B_NOPE + B_V)
    wkb = jnp.pad(wkv3[:, :, :B_NOPE], ((0, 0), (0, 0), (0, LANES - B_NOPE))
                  ).reshape(B_KV_RANK, hw).astype(BF16)
    wvt = wkv3[:, :, B_NOPE:].reshape(B_KV_RANK, B_HEADS * B_V).T.astype(BF16)
    vrows = B_V + ONES_ROWS

    q, k, vt = pl.pallas_call(
        functools.partial(_proj_mla_kernel, tk=tk),
        grid=(b, s // tm),
        in_specs=[
            pl.BlockSpec((None, tm, d), lambda bi, i: (bi, i, 0)),
            _const_spec((1, d)),
            _const_spec((d, B_Q_RANK)),
            _const_spec((d, B_KV_RANK)),
            _const_spec((d, LANES)),
            _const_spec((1, B_Q_RANK)),
            _const_spec((1, B_KV_RANK)),
            _const_spec((B_Q_RANK, hw)),
            _const_spec((B_KV_RANK, hw)),
            _const_spec((B_HEADS * B_V, B_KV_RANK)),
            pl.BlockSpec((6, tm, LANES), lambda bi, i: (0, i, 0)),
        ],
        out_specs=[
            pl.BlockSpec((None, tm, hw), lambda bi, i: (bi, i, 0)),
            pl.BlockSpec((None, tm, hw), lambda bi, i: (bi, i, 0)),
            pl.BlockSpec((None, B_HEADS, tm // tk, vrows, tk), lambda bi, i: (bi, 0, i, 0, 0)),
        ],
        out_shape=[
            jax.ShapeDtypeStruct((b, s, hw), BF16),
            jax.ShapeDtypeStruct((b, s, hw), BF16),
            jax.ShapeDtypeStruct((b, B_HEADS, nck, vrows, tk), BF16),
        ],
        compiler_params=_params(("parallel", "parallel")),
        name="proj_mla",
    )(x, gain.reshape(1, d), waq, wakv, war, q_norm.reshape(1, -1), kv_norm.reshape(1, -1),
      wqb, wkb, wvt, tab)

    return pl.pallas_call(
        functools.partial(_attn_mla_kernel, tk=tk),
        grid=(b, B_HEADS // 2, s // tq),
        in_specs=[
            pl.BlockSpec((None, tq, 2 * LANES), lambda bi, h, i: (bi, i, h)),
            pl.BlockSpec((None, s, 2 * LANES), lambda bi, h, i: (bi, 0, h)),
            pl.BlockSpec((None, 2, nck, vrows, tk), lambda bi, h, i: (bi, h, 0, 0, 0)),
        ],
        out_specs=pl.BlockSpec((None, tq, 2 * B_V), lambda bi, h, i: (bi, i, h)),
        out_shape=jax.ShapeDtypeStruct((b, s, B_HEADS * B_V), BF16),
        scratch_shapes=[pltpu.VMEM((2, 8, tq), F32)],
        compiler_params=_params(("parallel", "parallel", "arbitrary")),
        name="attn_mla",
    )(q, k, vt)


def _proj_dil_kernel(x_ref, g_ref, w_ref, tab_ref, q_ref, k_ref, v_ref):
    h = _rms_rows(x_ref[...], g_ref[...]).astype(BF16)
    width = C_HEADS * C_HEAD_DIM
    for c in range(3 * width // 512):
        y = jnp.dot(h, w_ref[:, c * 512:(c + 1) * 512], preferred_element_type=F32)
        which = c // 2
        dst = (q_ref, k_ref, v_ref)[which]
        for j in range(4):
            lo = (c % 2) * 512 + j * LANES
            yj = y[:, j * LANES:(j + 1) * LANES]
            if which < 2:
                yj = _rope(yj, tab_ref, 3 * which, C_ROT // 2)
            dst[:, lo:lo + LANES] = yj.astype(BF16)


def _attn_dil_kernel(q_ref, kp_ref, kc_ref, kn_ref, vp_ref, vc_ref, vn_ref, o_ref, lse_ref, *, seq):
    tq = q_ref.shape[0]
    w = tq + 2 * C_HALO
    q0 = pl.program_id(2) * tq
    qpos = q0 + lax.broadcasted_iota(jnp.int32, (tq, w), 0)
    kpos = q0 - C_HALO + lax.broadcasted_iota(jnp.int32, (tq, w), 1)
    mask = (jnp.abs(qpos - kpos) <= C_HALO) & (kpos >= 0) & (kpos < seq)
    low = lax.broadcasted_iota(jnp.int32, (tq, LANES), 1) < C_HEAD_DIM
    head_lane = lax.broadcasted_iota(jnp.int32, (tq, LANES), 1)
    lse_all = jnp.zeros((tq, LANES), F32)
    for j in range(C_HEADS // 2):
        lanes = slice(j * LANES, (j + 1) * LANES)
        q2 = q_ref[:, lanes]
        zero = jnp.zeros_like(q2)
        k2 = jnp.concatenate([kp_ref[:, lanes], kc_ref[:, lanes], kn_ref[:, lanes]], axis=0)
        v2 = jnp.concatenate([vp_ref[:, lanes], vc_ref[:, lanes], vn_ref[:, lanes]], axis=0)
        halves = []
        for e in range(2):
            qm = jnp.where(low, q2, zero) if e == 0 else jnp.where(low, zero, q2)
            sc = lax.dot_general(qm, k2, NT_DIMS, preferred_element_type=F32)
            sc = jnp.where(mask, sc, NEG_BIG)
            m = jnp.max(sc, axis=-1, keepdims=True)
            p = jnp.exp2(sc - m)
            l = jnp.sum(p, axis=-1, keepdims=True)
            pv = jnp.dot(p.astype(BF16), v2, preferred_element_type=F32)
            halves.append(pv * (1.0 / l))
            lse_all = jnp.where(head_lane == 2 * j + e, m + jnp.log2(l), lse_all)
        o_ref[:, lanes] = jnp.where(low, halves[0], halves[1]).astype(o_ref.dtype)
    lse_ref[...] = lse_all


def _combine_dil_kernel(o1_ref, o2_ref, o3_ref, l1_ref, l2_ref, l3_ref, o_ref):
    ls = [l1_ref[...], l2_ref[...], l3_ref[...]]
    m = jnp.maximum(jnp.maximum(ls[0], ls[1]), ls[2])
    es = [jnp.exp2(l - m) for l in ls]
    inv = 1.0 / (es[0] + es[1] + es[2])
    width = C_HEADS * C_HEAD_DIM
    expand = (lax.broadcasted_iota(jnp.int32, (LANES, width), 1) // C_HEAD_DIM
              == lax.broadcasted_iota(jnp.int32, (LANES, width), 0)).astype(BF16)
    acc = jnp.zeros(o_ref.shape, F32)
    for e, oref in zip(es, (o1_ref, o2_ref, o3_ref)):
        wt = e * inv
        hi = wt.astype(BF16)
        lo = (wt - hi.astype(F32)).astype(BF16)
        wide = (jnp.dot(hi, expand, preferred_element_type=F32)
                + jnp.dot(lo, expand, preferred_element_type=F32))
        acc = acc + wide * oref[...].astype(F32)
    o_ref[...] = acc.astype(o_ref.dtype)


def _dilated_attention(x, gain, w_qkv, cos, sin):
    b, s, d = x.shape
    width = C_HEADS * C_HEAD_DIM
    scale = C_HEAD_DIM ** -0.5 * LOG2E
    tab = jnp.concatenate([_lane_tables(cos, sin, C_HEAD_DIM, 0, scale),
                           _lane_tables(cos, sin, C_HEAD_DIM, 0, 1.0)])
    w_bf = w_qkv.astype(BF16)
    outs, lses = [], []
    for g, (window, dil) in enumerate(C_GROUPS):
        assert (window // 2) // dil == C_HALO
        seq = s // dil
        assert seq % C_HALO == 0
        tl = _tile(seq, 512)
        xg = x.reshape(b, seq, dil * d)
        tabg = tab.reshape(6, seq, dil * LANES)
        q, k, v = pl.pallas_call(
            _proj_dil_kernel,
            grid=(b, dil, seq // tl),
            in_specs=[
                pl.BlockSpec((None, tl, d), lambda bi, p, i: (bi, i, p)),
                _const_spec((1, d)),
                _const_spec((d, 3 * width)),
                pl.BlockSpec((6, tl, LANES), lambda bi, p, i: (0, i, p)),
            ],
            out_specs=[pl.BlockSpec((None, None, tl, width), lambda bi, p, i: (bi, p, i, 0))] * 3,
            out_shape=[jax.ShapeDtypeStruct((b, dil, seq, width), BF16)] * 3,
            compiler_params=_params(("parallel", "parallel", "parallel")),
            name=f"proj_dil{g}",
        )(xg, gain.reshape(1, d), w_bf[:, g * 3 * width:(g + 1) * 3 * width], tabg)

        tq = _tile(seq, 256)
        r = tq // C_HALO
        nhalo = seq // C_HALO
        cur = pl.BlockSpec((None, None, tq, width), lambda bi, p, i: (bi, p, i, 0))
        prev = pl.BlockSpec((None, None, C_HALO, width),
                            lambda bi, p, i: (bi, p, jnp.maximum(i * r - 1, 0), 0))
        nxt = pl.BlockSpec((None, None, C_HALO, width),
                           lambda bi, p, i: (bi, p, jnp.minimum((i + 1) * r, nhalo - 1), 0))
        o_g, lse_g = pl.pallas_call(
            functools.partial(_attn_dil_kernel, seq=seq),
            grid=(b, dil, seq // tq),
            in_specs=[cur, prev, cur, nxt, prev, cur, nxt],
            out_specs=[cur, pl.BlockSpec((None, None, tq, LANES), lambda bi, p, i: (bi, p, i, 0))],
            out_shape=[jax.ShapeDtypeStruct((b, dil, seq, width), BF16),
                       jax.ShapeDtypeStruct((b, dil, seq, LANES), F32)],
            compiler_params=_params(("parallel", "parallel", "parallel")),
            name=f"attn_dil{g}",
        )(q, k, k, k, v, v, v)
        outs.append(o_g)
        lses.append(lse_g)

    dmax = C_GROUPS[-1][1]
    rows = s // dmax
    ins, specs = [], []
    for cols, arrs in ((width, outs), (LANES, lses)):
        for arr, (_, dil) in zip(arrs, C_GROUPS):
            sub = dmax // dil
            ins.append(arr.reshape(b, dil, rows, sub * cols))
            specs.append(pl.BlockSpec((None, None, rows, cols),
                                      lambda bi, p, dil=dil: (bi, p % dil, 0, p // dil)))
    o = pl.pallas_call(
        _combine_dil_kernel,
        grid=(b, dmax),
        in_specs=specs,
        out_specs=pl.BlockSpec((None, rows, width), lambda bi, p: (bi, 0, p)),
        out_shape=jax.ShapeDtypeStruct((b, rows, dmax * width), BF16),
        compiler_params=_params(("parallel", "parallel")),
        name="combine_dil",
    )(*ins)
    return o.reshape(b, s, width)


def _post_kernel(o_ref, x_ref, wo_ref, g_ref, wgu_ref, wout_ref, fg_ref, y_ref, acc_ref, *,
                 apply_final):
    x1 = x_ref[...] + jnp.dot(o_ref[...], wo_ref[...], preferred_element_type=F32)
    h = _rms_rows(x1, g_ref[...]).astype(BF16)
    acc_ref[...] = x1
    for c in range(FFN_DIM // FFN_CHUNK):
        lo = c * FFN_CHUNK
        gate = jnp.dot(h, wgu_ref[:, lo:lo + FFN_CHUNK], preferred_element_type=F32)
        up = jnp.dot(h, wgu_ref[:, FFN_DIM + lo:FFN_DIM + lo + FFN_CHUNK], preferred_element_type=F32)
        act = (gate * jax.nn.sigmoid(gate) * up).astype(BF16)
        acc_ref[...] += jnp.dot(act, wout_ref[lo:lo + FFN_CHUNK, :], preferred_element_type=F32)
    y = acc_ref[...]
    if apply_final:
        y = _rms_rows(y, fg_ref[...])
    y_ref[...] = y


def _post(o, x, w_o, ffn_gain, w_gu, w_out, final_gain, apply_final):
    b, s, d = x.shape
    m = b * s
    tm = _tile(m, 512)
    row = lambda i: (i, 0)
    y = pl.pallas_call(
        functools.partial(_post_kernel, apply_final=apply_final),
        grid=(m // tm,),
        in_specs=[
            pl.BlockSpec((tm, d), row),
            pl.BlockSpec((tm, d), row),
            _const_spec((d, d)),
            _const_spec((1, d)),
            _const_spec((d, 2 * FFN_DIM)),
            _const_spec((FFN_DIM, d)),
            _const_spec((1, d)),
        ],
        out_specs=pl.BlockSpec((tm, d), row),
        out_shape=jax.ShapeDtypeStruct((m, d), F32),
        scratch_shapes=[pltpu.VMEM((tm, d), F32)],
        compiler_params=_params(("parallel",)),
        name="post",
    )(o.reshape(m, d), x.reshape(m, d), w_o.astype(BF16), ffn_gain.reshape(1, d),
      w_gu.astype(BF16), w_out.astype(BF16), final_gain.reshape(1, d))
    return y.reshape(b, s, d)


def kernel(x, attn_norm, ffn_norm, final_norm, a_w_qkv, a_lambda_q1, a_lambda_k1, a_lambda_q2,
           a_lambda_k2, a_subln, a_w_o, b_w_a, b_q_norm, b_kv_norm, b_w_qb, b_w_kvb, b_w_o,
           c_w_qkv, c_w_o, f_w_gu, f_w_out):
    depth = attn_norm.shape[0]
    s = x.shape[1]
    cos_p, sin_p = _rope_tables(s, A_ROT)
    cos_b, sin_b = _rope_tables(s, B_ROPE)
    for i in range(depth):
        mixer, j = i % N_MIXERS, i // N_MIXERS
        if mixer == 0:
            lambda_init = 0.8 - 0.6 * math.exp(-0.3 * i)
            o = _diff_attention(x, attn_norm[i], a_w_qkv[j], a_lambda_q1[j], a_lambda_k1[j],
                                a_lambda_q2[j], a_lambda_k2[j], a_subln[j], lambda_init, cos_p, sin_p)
            w_o = a_w_o[j]
        elif mixer == 1:
            o = _latent_attention(x, attn_norm[i], b_w_a[j], b_q_norm[j], b_kv_norm[j], b_w_qb[j],
                                  b_w_kvb[j], cos_b, sin_b)
            w_o = b_w_o[j]
        else:
            o = _dilated_attention(x, attn_norm[i], c_w_qkv[j], cos_p, sin_p)
            w_o = c_w_o[j]
        x = _post(o, x, w_o, ffn_norm[i], f_w_gu[i], f_w_out[i], final_norm, i == depth - 1)
    return x
```

```python
import functools
import math

import jax
import jax.numpy as jnp
from jax import lax
from jax.experimental import pallas as pl
from jax.experimental.pallas import tpu as pltpu

F32 = jnp.float32
BF16 = jnp.bfloat16

D_MODEL = 1024
N_MIXERS = 3
ROPE_THETA = 500000.0
EPS = 1e-6
NEG_BIG = -1e30
LOG2E = 1.4426950408889634

A_HEADS = 8
A_HEAD_DIM = 64
A_V_DIM = 128
A_ROT = 16

B_HEADS = 16
B_Q_RANK = 384
B_KV_RANK = 256
B_NOPE = 64
B_ROPE = 32
B_V = 64

C_GROUPS = ((128, 1), (512, 4), (2048, 16))
C_HEADS = 16
C_HEAD_DIM = 64
C_ROT = 16
C_HALO = 64

FFN_DIM = 2816
FFN_CHUNK = 256

LANES = 128
ONES_ROWS = 16
VMEM_LIMIT_BYTES = 56 * 1024 * 1024

NT_DIMS = (((1,), (1,)), ((), ()))
BOUND_MARGIN = 1.01
MIN_DENOMINATOR = 2.0 ** -60


def _tile(n, pref):
    t = min(n, pref)
    assert n % t == 0, (n, t)
    return t


def _params(sem):
    return pltpu.CompilerParams(dimension_semantics=sem, vmem_limit_bytes=VMEM_LIMIT_BYTES)


def _const_spec(shape):
    nd = len(shape)
    return pl.BlockSpec(shape, lambda *_: (0,) * nd, pipeline_mode=pl.Buffered(1))


def _rms_rows(x, gain):
    return x * lax.rsqrt(jnp.mean(x * x, axis=-1, keepdims=True) + EPS) * gain


def _rope_tables(seq_len, rot_dim):
    pos = jnp.arange(seq_len, dtype=F32)
    inv = ROPE_THETA ** (-jnp.arange(0, rot_dim, 2, dtype=F32) / rot_dim)
    ang = pos[:, None] * inv[None, :]
    return jnp.cos(ang), jnp.sin(ang)


def _lane_tables(cos, sin, period, offset, scale):
    s, r = cos.shape
    c = jnp.ones((s, period), F32)
    c = c.at[:, offset:offset + r].set(cos).at[:, offset + r:offset + 2 * r].set(cos)
    sa = jnp.zeros((s, period), F32).at[:, offset:offset + r].set(-sin)
    sb = jnp.zeros((s, period), F32).at[:, offset + r:offset + 2 * r].set(sin)
    rep = LANES // period
    return jnp.stack([jnp.tile(t, (1, rep)) for t in (c, sa, sb)]) * scale


def _rope(x, tab_ref, base, r):
    return (x * tab_ref[base]
            + pltpu.roll(x, LANES - r, 1) * tab_ref[base + 1]
            + pltpu.roll(x, r, 1) * tab_ref[base + 2])


def _softmax_t(chains, k_ref, vt_ref, tk, shifts=None):
    rows = vt_ref.shape[-2]
    nchunk = vt_ref.shape[-3]
    state = [(jnp.full((1, qm.shape[0]), NEG_BIG, F32), jnp.zeros((rows, qm.shape[0]), F32))
             for qm, _, _ in chains]
    items = [(c, n) for c in range(nchunk) for n in range(len(chains))]

    def scores(item):
        c, n = item
        qm, k_lanes, _ = chains[n]
        return lax.dot_general(k_ref[c * tk:(c + 1) * tk, k_lanes], qm, NT_DIMS,
                               preferred_element_type=F32)

    st = scores(items[0])
    for i, (c, n) in enumerate(items):
        st_next = scores(items[i + 1]) if i + 1 < len(items) else None
        m, acc = state[n]
        vt = vt_ref[chains[n][2] + (c,)]
        if shifts is None:
            m_new = jnp.maximum(m, jnp.max(st, axis=0, keepdims=True))
            pt = jnp.exp2(st - m_new).astype(BF16)
            acc = acc * jnp.exp2(m - m_new)
        else:
            m_new = m
            pt = jnp.exp2(st - shifts[n]).astype(BF16)
        state[n] = (m_new, acc + jnp.dot(vt, pt, preferred_element_type=F32))
        st = st_next
    return [acc for _, acc in state]


def _key_norm_bound(k_ref, k_lanes, sel, width):
    kk = k_ref[:, k_lanes]
    kn2 = lax.dot_general(sel, kk * kk, NT_DIMS, preferred_element_type=F32)
    return jnp.broadcast_to(jnp.sqrt(jnp.max(kn2, axis=-1, keepdims=True)), (8, width))


def _score_bound(qm, kmax):
    ones = jnp.ones((8, qm.shape[1]), BF16)
    qn2 = lax.dot_general(ones, qm * qm, NT_DIMS, preferred_element_type=F32)
    return jnp.sqrt(qn2[0:1]) * kmax[0:1] * BOUND_MARGIN


def _denominators_ok(accs, row):
    lmin = accs[0][row:row + 1]
    for acc in accs[1:]:
        lmin = jnp.minimum(lmin, acc[row:row + 1])
    return jnp.min(lmin) >= MIN_DENOMINATOR


def _proj_diff_kernel(x_ref, g_ref, wqk_ref, wvt_ref, tab_ref, q_ref, k_ref, vt_ref, *, tk):
    tm = x_ref.shape[0]
    h = _rms_rows(x_ref[...], g_ref[...]).astype(BF16)
    for c in range(4):
        y = jnp.dot(h, wqk_ref[:, c * 512:(c + 1) * 512], preferred_element_type=F32)
        dst, base = (q_ref, 0) if c < 2 else (k_ref, 3)
        for j in range(4):
            lo = (c % 2) * 512 + j * LANES
            dst[:, lo:lo + LANES] = _rope(y[:, j * LANES:(j + 1) * LANES], tab_ref, base,
                                          A_ROT // 2).astype(BF16)
    for half in range(2):
        vt = lax.dot_general(wvt_ref[half * 512:(half + 1) * 512, :], h, NT_DIMS,
                             preferred_element_type=F32)
        for hh in range(4):
            hv = half * 4 + hh
            for c in range(tm // tk):
                vt_ref[hv, c, 0:A_V_DIM, :] = vt[hh * A_V_DIM:(hh + 1) * A_V_DIM,
                                                 c * tk:(c + 1) * tk].astype(BF16)
                vt_ref[hv, c, A_V_DIM:A_V_DIM + ONES_ROWS, :] = jnp.ones((ONES_ROWS, tk), BF16)


def _attn_diff_kernel(lam_ref, sub_ref, q_ref, k_ref, vt_ref, o_ref, kmax_ref, *, tk, lambda_init):
    q2 = q_ref[...]
    low = lax.broadcasted_iota(jnp.int32, q2.shape, 1) < A_HEAD_DIM
    zero = jnp.zeros_like(q2)
    chains = [(jnp.where(low, q2, zero), slice(None), ()), (jnp.where(low, zero, q2), slice(None), ())]

    @pl.when(pl.program_id(2) == 0)
    def _():
        sel_low = lax.broadcasted_iota(jnp.int32, (8, LANES), 1) < A_HEAD_DIM
        for n in range(2):
            sel = (sel_low if n == 0 else jnp.logical_not(sel_low)).astype(BF16)
            kmax_ref[n] = _key_norm_bound(k_ref, slice(None), sel, kmax_ref.shape[-1])

    l1 = jnp.sum(lam_ref[0:1, :] * lam_ref[1:2, :], axis=-1, keepdims=True)
    l2 = jnp.sum(lam_ref[2:3, :] * lam_ref[3:4, :], axis=-1, keepdims=True)
    lam = jnp.exp(l1) - jnp.exp(l2) + lambda_init

    def finish(accs):
        outs = [acc[0:A_V_DIM] * (1.0 / acc[A_V_DIM:A_V_DIM + 1]) for acc in accs]
        o = (outs[0] - lam * outs[1]).T
        o_ref[...] = (_rms_rows(o, sub_ref[...]) * (1.0 - lambda_init)).astype(o_ref.dtype)

    shifts = [_score_bound(chains[n][0], kmax_ref[n]) for n in range(2)]
    accs = _softmax_t(chains, k_ref, vt_ref, tk, shifts)
    finish(accs)

    @pl.when(jnp.logical_not(_denominators_ok(accs, A_V_DIM)))
    def _():
        finish(_softmax_t(chains, k_ref, vt_ref, tk))


def _diff_attention(x, gain, w_qkv, lam_q1, lam_k1, lam_q2, lam_k2, subln, lambda_init, cos, sin):
    b, s, d = x.shape
    tm = _tile(s, 512)
    tq = _tile(s, 512)
    tk = _tile(s, 512)
    tm = max(tm, tk)
    nck = s // tk
    qk_w = 2 * A_HEADS * A_HEAD_DIM
    scale = A_HEAD_DIM ** -0.5 * LOG2E
    tab = jnp.concatenate([_lane_tables(cos, sin, A_HEAD_DIM, 0, scale),
                           _lane_tables(cos, sin, A_HEAD_DIM, 0, 1.0)])
    wqk = w_qkv[:, :2 * qk_w].astype(BF16)
    wvt = w_qkv[:, 2 * qk_w:].T.astype(BF16)
    vrows = A_V_DIM + ONES_ROWS

    q, k, vt = pl.pallas_call(
        functools.partial(_proj_diff_kernel, tk=tk),
        grid=(b, s // tm),
        in_specs=[
            pl.BlockSpec((None, tm, d), lambda bi, i: (bi, i, 0)),
            _const_spec((1, d)),
            _const_spec((d, 2 * qk_w)),
            _const_spec((A_HEADS * A_V_DIM, d)),
            pl.BlockSpec((6, tm, LANES), lambda bi, i: (0, i, 0)),
        ],
        out_specs=[
            pl.BlockSpec((None, tm, qk_w), lambda bi, i: (bi, i, 0)),
            pl.BlockSpec((None, tm, qk_w), lambda bi, i: (bi, i, 0)),
            pl.BlockSpec((None, A_HEADS, tm // tk, vrows, tk), lambda bi, i: (bi, 0, i, 0, 0)),
        ],
        out_shape=[
            jax.ShapeDtypeStruct((b, s, qk_w), BF16),
            jax.ShapeDtypeStruct((b, s, qk_w), BF16),
            jax.ShapeDtypeStruct((b, A_HEADS, nck, vrows, tk), BF16),
        ],
        compiler_params=_params(("parallel", "parallel")),
        name="proj_diff",
    )(x, gain.reshape(1, d), wqk, wvt, tab)

    lam = jnp.stack([lam_q1, lam_k1, lam_q2, lam_k2]).astype(F32)
    return pl.pallas_call(
        functools.partial(_attn_diff_kernel, tk=tk, lambda_init=lambda_init),
        grid=(b, A_HEADS, s // tq),
        in_specs=[
            _const_spec((4, A_HEAD_DIM)),
            _const_spec((1, A_V_DIM)),
            pl.BlockSpec((None, tq, LANES), lambda bi, h, i: (bi, i, h)),
            pl.BlockSpec((None, s, LANES), lambda bi, h, i: (bi, 0, h)),
            pl.BlockSpec((None, None, nck, vrows, tk), lambda bi, h, i: (bi, h, 0, 0, 0)),
        ],
        out_specs=pl.BlockSpec((None, tq, A_V_DIM), lambda bi, h, i: (bi, i, h)),
        out_shape=jax.ShapeDtypeStruct((b, s, A_HEADS * A_V_DIM), BF16),
        scratch_shapes=[pltpu.VMEM((2, 8, tq), F32)],
        compiler_params=_params(("parallel", "parallel", "arbitrary")),
        name="attn_diff",
    )(lam, subln.reshape(1, A_V_DIM).astype(F32), q, k, vt)


def _proj_mla_kernel(x_ref, g_ref, waq_ref, wakv_ref, war_ref, qn_ref, kvn_ref, wqb_ref, wkb_ref,
                     wvt_ref, tab_ref, q_ref, k_ref, vt_ref, *, tk):
    tm = x_ref.shape[0]
    r = B_ROPE // 2
    h = _rms_rows(x_ref[...], g_ref[...]).astype(BF16)
    q_lat = _rms_rows(jnp.dot(h, waq_ref[...], preferred_element_type=F32), qn_ref[...]).astype(BF16)
    kv_lat = _rms_rows(jnp.dot(h, wakv_ref[...], preferred_element_type=F32), kvn_ref[...]).astype(BF16)
    k_rope = _rope(jnp.dot(h, war_ref[...], preferred_element_type=F32), tab_ref, 3, r)
    for c in range(4):
        yq = jnp.dot(q_lat, wqb_ref[:, c * 512:(c + 1) * 512], preferred_element_type=F32)
        yk = jnp.dot(kv_lat, wkb_ref[:, c * 512:(c + 1) * 512], preferred_element_type=F32)
        for j in range(4):
            lo = c * 512 + j * LANES
            q_ref[:, lo:lo + LANES] = _rope(yq[:, j * LANES:(j + 1) * LANES], tab_ref, 0, r).astype(BF16)
            k_ref[:, lo:lo + LANES] = (yk[:, j * LANES:(j + 1) * LANES] + k_rope).astype(BF16)
    for half in range(2):
        vt = lax.dot_general(wvt_ref[half * 512:(half + 1) * 512, :], kv_lat, NT_DIMS,
                             preferred_element_type=F32)
        for hh in range(8):
            hv = half * 8 + hh
            for c in range(tm // tk):
                vt_ref[hv, c, 0:B_V, :] = vt[hh * B_V:(hh + 1) * B_V, c * tk:(c + 1) * tk].astype(BF16)
                vt_ref[hv, c, B_V:B_V + ONES_ROWS, :] = jnp.ones((ONES_ROWS, tk), BF16)


def _attn_mla_kernel(q_ref, k_ref, vt_ref, o_ref, kmax_ref, *, tk):
    chains = [(q_ref[:, e * LANES:(e + 1) * LANES], slice(e * LANES, (e + 1) * LANES), (e,))
              for e in range(2)]

    @pl.when(pl.program_id(2) == 0)
    def _():
        sel = jnp.ones((8, LANES), BF16)
        for n in range(2):
            kmax_ref[n] = _key_norm_bound(k_ref, chains[n][1], sel, kmax_ref.shape[-1])

    def finish(accs):
        outs = [acc[0:B_V] * (1.0 / acc[B_V:B_V + 1]) for acc in accs]
        o_ref[...] = jnp.concatenate(outs, axis=0).T.astype(o_ref.dtype)

    shifts = [_score_bound(chains[n][0], kmax_ref[n]) for n in range(2)]
    accs = _softmax_t(chains, k_ref, vt_ref, tk, shifts)
    finish(accs)

    @pl.when(jnp.logical_not(_denominators_ok(accs, B_V)))
    def _():
        finish(_softmax_t(chains, k_ref, vt_ref, tk))


def _latent_attention(x, gain, w_a, q_norm, kv_norm, w_qb, w_kvb, cos, sin):
    b, s, d = x.shape
    tq = _tile(s, 512)
    tk = _tile(s, 512)
    tm = tk
    nck = s // tk
    hw = B_HEADS * LANES
    scale = (B_NOPE + B_ROPE) ** -0.5 * LOG2E
    tab = jnp.concatenate([_lane_tables(cos, sin, LANES, B_NOPE, scale),
                           _lane_tables(cos, sin, LANES, B_NOPE, 1.0)])
    waq = w_a[:, :B_Q_RANK].astype(BF16)
    wakv = w_a[:, B_Q_RANK:B_Q_RANK + B_KV_RANK].astype(BF16)
    war = jnp.zeros((d, LANES), F32).at[:, B_NOPE:B_NOPE + B_ROPE].set(
        w_a[:, B_Q_RANK + B_KV_RANK:]).astype(BF16)
    pad = LANES - (B_NOPE + B_ROPE)
    wqb = jnp.pad(w_qb.reshape(B_Q_RANK, B_HEADS, B_NOPE + B_ROPE),
                  ((0, 0), (0, 0), (0, pad))).reshape(B_Q_RANK, hw).astype(BF16)
    wkv3 = w_kvb.reshape(B_KV_RANK, B_HEADS, B_NOPE + B_V)
    wkb = jnp.pad(wkv3[:, :, :B_NOPE], ((0, 0), (0, 0), (0, LANES - B_NOPE))
                  ).reshape(B_KV_RANK, hw).astype(BF16)
    wvt = wkv3[:, :, B_NOPE:].reshape(B_KV_RANK, B_HEADS * B_V).T.astype(BF16)
    vrows = B_V + ONES_ROWS

    q, k, vt = pl.pallas_call(
        functools.partial(_proj_mla_kernel, tk=tk),
        grid=(b, s // tm),
        in_specs=[
            pl.BlockSpec((None, tm, d), lambda bi, i: (bi, i, 0)),
            _const_spec((1, d)),
            _const_spec((d, B_Q_RANK)),
            _const_spec((d, B_KV_RANK)),
            _const_spec((d, LANES)),
            _const_spec((1, B_Q_RANK)),
            _const_spec((1, B_KV_RANK)),
            _const_spec((B_Q_RANK, hw)),
            _const_spec((B_KV_RANK, hw)),
            _const_spec((B_HEADS * B_V, B_KV_RANK)),
            pl.BlockSpec((6, tm, LANES), lambda bi, i: (0, i, 0)),
        ],
        out_specs=[
            pl.BlockSpec((None, tm, hw), lambda bi, i: (bi, i, 0)),
            pl.BlockSpec((None, tm, hw), lambda bi, i: (bi, i, 0)),
            pl.BlockSpec((None, B_HEADS, tm // tk, vrows, tk), lambda bi, i: (bi, 0, i, 0, 0)),
        ],
        out_shape=[
            jax.ShapeDtypeStruct((b, s, hw), BF16),
            jax.ShapeDtypeStruct((b, s, hw), BF16),
            jax.ShapeDtypeStruct((b, B_HEADS, nck, vrows, tk), BF16),
        ],
        compiler_params=_params(("parallel", "parallel")),
        name="proj_mla",
    )(x, gain.reshape(1, d), waq, wakv, war, q_norm.reshape(1, -1), kv_norm.reshape(1, -1),
      wqb, wkb, wvt, tab)

    return pl.pallas_call(
        functools.partial(_attn_mla_kernel, tk=tk),
        grid=(b, B_HEADS // 2, s // tq),
        in_specs=[
            pl.BlockSpec((None, tq, 2 * LANES), lambda bi, h, i: (bi, i, h)),
            pl.BlockSpec((None, s, 2 * LANES), lambda bi, h, i: (bi, 0, h)),
            pl.BlockSpec((None, 2, nck, vrows, tk), lambda bi, h, i: (bi, h, 0, 0, 0)),
        ],
        out_specs=pl.BlockSpec((None, tq, 2 * B_V), lambda bi, h, i: (bi, i, h)),
        out_shape=jax.ShapeDtypeStruct((b, s, B_HEADS * B_V), BF16),
        scratch_shapes=[pltpu.VMEM((2, 8, tq), F32)],
        compiler_params=_params(("parallel", "parallel", "arbitrary")),
        name="attn_mla",
    )(q, k, vt)


def _proj_dil_kernel(x_ref, g_ref, w_ref, tab_ref, *refs):
    outs, scr = refs[:-1], refs[-1]
    tl = x_ref.shape[0]
    h = _rms_rows(x_ref[...], g_ref[...]).astype(BF16)
    width = C_HEADS * C_HEAD_DIM
    for g, (_, dil) in enumerate(C_GROUPS):
        for c in range(3 * width // 512):
            col = g * 3 * width + c * 512
            y = jnp.dot(h, w_ref[:, col:col + 512], preferred_element_type=F32)
            which = c // 2
            dst = outs[3 * g + which]
            lo = (c % 2) * 512
            for j in range(4):
                yj = y[:, j * LANES:(j + 1) * LANES]
                if which < 2:
                    yj = _rope(yj, tab_ref, 3 * which, C_ROT // 2)
                cols = slice(lo + j * LANES, lo + (j + 1) * LANES)
                if dil == 1:
                    dst[0, :, cols] = yj.astype(BF16)
                    continue
                scr[j] = yj
                for p in range(dil):
                    dst[p, :, cols] = scr[j, pl.ds(p, tl // dil, stride=dil), :].astype(BF16)


def _attn_dil_kernel(q_ref, kp_ref, kc_ref, kn_ref, vp_ref, vc_ref, vn_ref, o_ref, lse_ref, *, seq, tq):
    tt = q_ref.shape[0]
    nsub = tt // tq
    w = tq + 2 * C_HALO
    t0 = pl.program_id(2) * tt
    col = lax.broadcasted_iota(jnp.int32, (tq, w), 1)
    band = jnp.abs(col - C_HALO - lax.broadcasted_iota(jnp.int32, (tq, w), 0)) <= C_HALO
    head_lane = lax.broadcasted_iota(jnp.int32, (tq, LANES), 1)
    low = head_lane < C_HEAD_DIM
    cache = {}

    def window(prev_ref, cur_ref, next_ref, sub, lanes):
        lo, hi = sub * tq - C_HALO, (sub + 1) * tq + C_HALO
        parts = []
        if lo < 0:
            parts.append(prev_ref[:, lanes])
        parts.append(cur_ref[max(lo, 0):min(hi, tt), lanes])
        if hi > tt:
            parts.append(next_ref[:, lanes])
        return parts[0] if len(parts) == 1 else jnp.concatenate(parts, axis=0)

    def operands(sub, j):
        if (sub, j) not in cache:
            lanes = slice(j * LANES, (j + 1) * LANES)
            cache[(sub, j)] = (q_ref[sub * tq:(sub + 1) * tq, lanes],
                               window(kp_ref, kc_ref, kn_ref, sub, lanes),
                               window(vp_ref, vc_ref, vn_ref, sub, lanes))
        return cache[(sub, j)]

    def mask(sub):
        if ("mask", sub) not in cache:
            kpos = t0 + sub * tq - C_HALO + col
            cache[("mask", sub)] = band & (kpos >= 0) & (kpos < seq)
        return cache[("mask", sub)]

    def scores(unit):
        sub, j, e = unit
        q2 = operands(sub, j)[0]
        zero = jnp.zeros_like(q2)
        qm = jnp.where(low, q2, zero) if e == 0 else jnp.where(low, zero, q2)
        return lax.dot_general(qm, operands(sub, j)[1], NT_DIMS, preferred_element_type=F32)

    units = [(sub, j, e) for sub in range(nsub) for j in range(C_HEADS // 2) for e in range(2)]
    sc = scores(units[0])
    halves, lse_all = [], None
    for i, (sub, j, e) in enumerate(units):
        sc_next = scores(units[i + 1]) if i + 1 < len(units) else None
        sc = jnp.where(mask(sub), sc, NEG_BIG)
        m = jnp.max(sc, axis=-1, keepdims=True)
        p = jnp.exp2(sc - m)
        l = jnp.sum(p, axis=-1, keepdims=True)
        pv = jnp.dot(p.astype(BF16), operands(sub, j)[2], preferred_element_type=F32)
        halves.append(pv * (1.0 / l))
        lse = m + jnp.log2(l)
        lse_all = lse + jnp.zeros((tq, LANES), F32) if (j, e) == (0, 0) else jnp.where(
            head_lane == 2 * j + e, lse, lse_all)
        if e == 1:
            rows = slice(sub * tq, (sub + 1) * tq)
            o_ref[rows, j * LANES:(j + 1) * LANES] = jnp.where(low, halves[0], halves[1]).astype(o_ref.dtype)
            halves = []
            if j == C_HEADS // 2 - 1:
                lse_ref[rows, :] = lse_all
        sc = sc_next


def _combine_dil_kernel(o1_ref, o2_ref, o3_ref, l1_ref, l2_ref, l3_ref, o_ref, so_ref, sl_ref):
    rows = o_ref.shape[0]
    nlg = o_ref.shape[1] // LANES
    os_, ls = [], []
    for g, ((_, dil), oref, lref) in enumerate(zip(C_GROUPS, (o1_ref, o2_ref, o3_ref),
                                                   (l1_ref, l2_ref, l3_ref))):
        if dil == 1:
            os_.append(oref[0].astype(F32))
            ls.append(lref[0])
            continue
        for p in range(dil):
            og = oref[p].astype(F32)
            for j in range(nlg):
                so_ref[g - 1, j, pl.ds(p, rows // dil, stride=dil), :] = og[:, j * LANES:(j + 1) * LANES]
            sl_ref[g - 1, pl.ds(p, rows // dil, stride=dil), :] = lref[p]
        os_.append(jnp.concatenate([so_ref[g - 1, j] for j in range(nlg)], axis=-1))
        ls.append(sl_ref[g - 1])
    m = jnp.maximum(jnp.maximum(ls[0], ls[1]), ls[2])
    es = [jnp.exp2(l - m) for l in ls]
    inv = 1.0 / (es[0] + es[1] + es[2])
    width = C_HEADS * C_HEAD_DIM
    expand = (lax.broadcasted_iota(jnp.int32, (LANES, width), 1) // C_HEAD_DIM
              == lax.broadcasted_iota(jnp.int32, (LANES, width), 0)).astype(BF16)
    acc = jnp.zeros(o_ref.shape, F32)
    for e, og in zip(es, os_):
        wt = e * inv
        hi = wt.astype(BF16)
        lo = (wt - hi.astype(F32)).astype(BF16)
        wide = (jnp.dot(hi, expand, preferred_element_type=F32)
                + jnp.dot(lo, expand, preferred_element_type=F32))
        acc = acc + wide * og
    o_ref[...] = acc.astype(o_ref.dtype)


def _dilated_attention(x, gain, w_qkv, cos, sin):
    b, s, d = x.shape
    width = C_HEADS * C_HEAD_DIM
    scale = C_HEAD_DIM ** -0.5 * LOG2E
    tab = jnp.concatenate([_lane_tables(cos, sin, C_HEAD_DIM, 0, scale),
                           _lane_tables(cos, sin, C_HEAD_DIM, 0, 1.0)])
    dmax = max(dil for _, dil in C_GROUPS)
    for window, dil in C_GROUPS:
        assert (window // 2) // dil == C_HALO and (s // dil) % C_HALO == 0

    tl = _tile(s, 256)
    assert tl % (ONES_ROWS * dmax) == 0
    phase_spec = lambda dil, cols: pl.BlockSpec((None, dil, tl // dil, cols),
                                                lambda bi, i: (bi, 0, i, 0))
    qkv = pl.pallas_call(
        _proj_dil_kernel,
        grid=(b, s // tl),
        in_specs=[
            pl.BlockSpec((None, tl, d), lambda bi, i: (bi, i, 0)),
            _const_spec((1, d)),
            _const_spec((d, len(C_GROUPS) * 3 * width)),
            pl.BlockSpec((6, tl, LANES), lambda bi, i: (0, i, 0)),
        ],
        out_specs=[phase_spec(dil, width) for _, dil in C_GROUPS for _ in range(3)],
        out_shape=[jax.ShapeDtypeStruct((b, dil, s // dil, width), BF16)
                   for _, dil in C_GROUPS for _ in range(3)],
        scratch_shapes=[pltpu.VMEM((4, tl, LANES), F32)],
        compiler_params=_params(("parallel", "parallel")),
        name="proj_dil",
    )(x, gain.reshape(1, d), w_qkv.astype(BF16), tab)

    outs, lses = [], []
    for g, (_, dil) in enumerate(C_GROUPS):
        q, k, v = qkv[3 * g:3 * g + 3]
        seq = s // dil
        tt = _tile(seq, 512)
        tq = _tile(tt, 128)
        r = tt // C_HALO
        nhalo = seq // C_HALO
        cur = pl.BlockSpec((None, None, tt, width), lambda bi, p, i: (bi, p, i, 0))
        prev = pl.BlockSpec((None, None, C_HALO, width),
                            lambda bi, p, i, r=r: (bi, p, jnp.maximum(i * r - 1, 0), 0))
        nxt = pl.BlockSpec((None, None, C_HALO, width),
                           lambda bi, p, i, r=r, nhalo=nhalo: (bi, p, jnp.minimum((i + 1) * r, nhalo - 1), 0))
        o_g, lse_g = pl.pallas_call(
            functools.partial(_attn_dil_kernel, seq=seq, tq=tq),
            grid=(b, dil, seq // tt),
            in_specs=[cur, prev, cur, nxt, prev, cur, nxt],
            out_specs=[cur, pl.BlockSpec((None, None, tt, LANES), lambda bi, p, i: (bi, p, i, 0))],
            out_shape=[jax.ShapeDtypeStruct((b, dil, seq, width), BF16),
                       jax.ShapeDtypeStruct((b, dil, seq, LANES), F32)],
            compiler_params=_params(("parallel", "parallel", "parallel")),
            name=f"attn_dil{g}",
        )(q, k, k, k, v, v, v)
        outs.append(o_g)
        lses.append(lse_g)

    return pl.pallas_call(
        _combine_dil_kernel,
        grid=(b, s // tl),
        in_specs=([phase_spec(dil, width) for _, dil in C_GROUPS]
                  + [phase_spec(dil, LANES) for _, dil in C_GROUPS]),
        out_specs=pl.BlockSpec((None, tl, width), lambda bi, i: (bi, i, 0)),
        out_shape=jax.ShapeDtypeStruct((b, s, width), BF16),
        scratch_shapes=[pltpu.VMEM((len(C_GROUPS) - 1, width // LANES, tl, LANES), F32),
                        pltpu.VMEM((len(C_GROUPS) - 1, tl, LANES), F32)],
        compiler_params=_params(("parallel", "parallel")),
        name="combine_dil",
    )(*outs, *lses)


def _post_kernel(o_ref, x_ref, wo_ref, g_ref, wgu_ref, wout_ref, fg_ref, y_ref, acc_ref, *,
                 apply_final):
    x1 = x_ref[...] + jnp.dot(o_ref[...], wo_ref[...], preferred_element_type=F32)
    h = _rms_rows(x1, g_ref[...]).astype(BF16)
    acc_ref[...] = x1
    for c in range(FFN_DIM // FFN_CHUNK):
        lo = c * FFN_CHUNK
        gate = jnp.dot(h, wgu_ref[:, lo:lo + FFN_CHUNK], preferred_element_type=F32)
        up = jnp.dot(h, wgu_ref[:, FFN_DIM + lo:FFN_DIM + lo + FFN_CHUNK], preferred_element_type=F32)
        act = (gate * jax.nn.sigmoid(gate) * up).astype(BF16)
        acc_ref[...] += jnp.dot(act, wout_ref[lo:lo + FFN_CHUNK, :], preferred_element_type=F32)
    y = acc_ref[...]
    if apply_final:
        y = _rms_rows(y, fg_ref[...])
    y_ref[...] = y


def _post(o, x, w_o, ffn_gain, w_gu, w_out, final_gain, apply_final):
    b, s, d = x.shape
    m = b * s
    tm = _tile(m, 512)
    row = lambda i: (i, 0)
    y = pl.pallas_call(
        functools.partial(_post_kernel, apply_final=apply_final),
        grid=(m // tm,),
        in_specs=[
            pl.BlockSpec((tm, d), row),
            pl.BlockSpec((tm, d), row),
            _const_spec((d, d)),
            _const_spec((1, d)),
            _const_spec((d, 2 * FFN_DIM)),
            _const_spec((FFN_DIM, d)),
            _const_spec((1, d)),
        ],
        out_specs=pl.BlockSpec((tm, d), row),
        out_shape=jax.ShapeDtypeStruct((m, d), F32),
        scratch_shapes=[pltpu.VMEM((tm, d), F32)],
        compiler_params=_params(("parallel",)),
        name="post",
    )(o.reshape(m, d), x.reshape(m, d), w_o.astype(BF16), ffn_gain.reshape(1, d),
      w_gu.astype(BF16), w_out.astype(BF16), final_gain.reshape(1, d))
    return y.reshape(b, s, d)


def kernel(x, attn_norm, ffn_norm, final_norm, a_w_qkv, a_lambda_q1, a_lambda_k1, a_lambda_q2,
           a_lambda_k2, a_subln, a_w_o, b_w_a, b_q_norm, b_kv_norm, b_w_qb, b_w_kvb, b_w_o,
           c_w_qkv, c_w_o, f_w_gu, f_w_out):
    depth = attn_norm.shape[0]
    s = x.shape[1]
    cos_p, sin_p = _rope_tables(s, A_ROT)
    cos_b, sin_b = _rope_tables(s, B_ROPE)
    for i in range(depth):
        mixer, j = i % N_MIXERS, i // N_MIXERS
        if mixer == 0:
            lambda_init = 0.8 - 0.6 * math.exp(-0.3 * i)
            o = _diff_attention(x, attn_norm[i], a_w_qkv[j], a_lambda_q1[j], a_lambda_k1[j],
                                a_lambda_q2[j], a_lambda_k2[j], a_subln[j], lambda_init, cos_p, sin_p)
            w_o = a_w_o[j]
        elif mixer == 1:
            o = _latent_attention(x, attn_norm[i], b_w_a[j], b_q_norm[j], b_kv_norm[j], b_w_qb[j],
                                  b_w_kvb[j], cos_b, sin_b)
            w_o = b_w_o[j]
        else:
            o = _dilated_attention(x, attn_norm[i], c_w_qkv[j], cos_p, sin_p)
            w_o = c_w_o[j]
        x = _post(o, x, w_o, ffn_norm[i], f_w_gu[i], f_w_out[i], final_norm, i == depth - 1)
    return x
```

```python
import functools
import math

import jax
import jax.numpy as jnp
from jax import lax
from jax.experimental import pallas as pl
from jax.experimental.pallas import tpu as pltpu

F32 = jnp.float32
BF16 = jnp.bfloat16

D_MODEL = 1024
N_MIXERS = 3
ROPE_THETA = 500000.0
EPS = 1e-6
NEG_BIG = -1e30
LOG2E = 1.4426950408889634

A_HEADS = 8
A_HEAD_DIM = 64
A_V_DIM = 128
A_ROT = 16

B_HEADS = 16
B_Q_RANK = 384
B_KV_RANK = 256
B_NOPE = 64
B_ROPE = 32
B_V = 64

C_GROUPS = ((128, 1), (512, 4), (2048, 16))
C_HEADS = 16
C_HEAD_DIM = 64
C_ROT = 16
C_HALO = 64

FFN_DIM = 2816
FFN_CHUNK = 256

LANES = 128
BF16_SUBLANES = 16
Q_UNROLL = 2
VMEM_LIMIT_BYTES = 56 * 1024 * 1024

NT_DIMS = (((1,), (1,)), ((), ()))
BOUND_MARGIN = 1.01
MIN_DENOMINATOR = 2.0 ** -60


def _tile(n, pref):
    t = min(n, pref)
    assert n % t == 0, (n, t)
    return t


def _params(sem):
    return pltpu.CompilerParams(dimension_semantics=sem, vmem_limit_bytes=VMEM_LIMIT_BYTES)


def _const_spec(shape):
    nd = len(shape)
    return pl.BlockSpec(shape, lambda *_: (0,) * nd, pipeline_mode=pl.Buffered(1))


def _rms_rows(x, gain):
    return x * lax.rsqrt(jnp.mean(x * x, axis=-1, keepdims=True) + EPS) * gain


def _rope_tables(seq_len, rot_dim):
    pos = jnp.arange(seq_len, dtype=F32)
    inv = ROPE_THETA ** (-jnp.arange(0, rot_dim, 2, dtype=F32) / rot_dim)
    ang = pos[:, None] * inv[None, :]
    return jnp.cos(ang), jnp.sin(ang)


def _lane_tables(cos, sin, period, offset, scale):
    s, r = cos.shape
    c = jnp.ones((s, period), F32)
    c = c.at[:, offset:offset + r].set(cos).at[:, offset + r:offset + 2 * r].set(cos)
    sa = jnp.zeros((s, period), F32).at[:, offset:offset + r].set(-sin)
    sb = jnp.zeros((s, period), F32).at[:, offset + r:offset + 2 * r].set(sin)
    rep = LANES // period
    return jnp.stack([jnp.tile(t, (1, rep)) for t in (c, sa, sb)]) * scale


def _rope(x, tab_ref, base, r):
    return (x * tab_ref[base]
            + pltpu.roll(x, LANES - r, 1) * tab_ref[base + 1]
            + pltpu.roll(x, r, 1) * tab_ref[base + 2])


def _softmax_t(chains, k_ref, vt_ref, tk, shifts=None):
    rows = vt_ref.shape[-2]
    nchunk = vt_ref.shape[-3]
    state = [(jnp.full((1, qm.shape[0]), NEG_BIG, F32), jnp.zeros((1, qm.shape[0]), F32),
              jnp.zeros((rows, qm.shape[0]), F32)) for qm, _, _ in chains]
    items = [(c, n) for c in range(nchunk) for n in range(len(chains))]

    def scores(item):
        c, n = item
        qm, k_lanes, _ = chains[n]
        return lax.dot_general(k_ref[c * tk:(c + 1) * tk, k_lanes], qm, NT_DIMS,
                               preferred_element_type=F32)

    st = scores(items[0])
    for i, (c, n) in enumerate(items):
        st_next = scores(items[i + 1]) if i + 1 < len(items) else None
        m, l, acc = state[n]
        if shifts is None:
            m_new = jnp.maximum(m, jnp.max(st, axis=0, keepdims=True))
            alpha = jnp.exp2(m - m_new)
            p = jnp.exp2(st - m_new)
            l, acc = l * alpha, acc * alpha
        else:
            m_new = m
            p = jnp.exp2(st - shifts[n])
        l = l + jnp.sum(p, axis=0, keepdims=True)
        acc = acc + jnp.dot(vt_ref[chains[n][2] + (c,)], p.astype(BF16), preferred_element_type=F32)
        state[n] = (m_new, l, acc)
        st = st_next
    return [(l, acc) for _, l, acc in state]


def _key_norm_bound(k_ref, k_lanes, sel, width):
    kk = k_ref[:, k_lanes]
    kn2 = lax.dot_general(sel, kk * kk, NT_DIMS, preferred_element_type=F32)
    return jnp.broadcast_to(jnp.sqrt(jnp.max(kn2, axis=-1, keepdims=True)), (8, width))


def _score_bound(qm, kmax):
    ones = jnp.ones((8, qm.shape[1]), BF16)
    qn2 = lax.dot_general(ones, qm * qm, NT_DIMS, preferred_element_type=F32)
    return jnp.sqrt(qn2[0:1]) * kmax[0:1] * BOUND_MARGIN


def _attend_tiles(tile_fn, n_tiles, tq):
    unroll = Q_UNROLL if n_tiles % Q_UNROLL == 0 else 1

    def body(it, carry):
        row0s = [pl.multiple_of((it * unroll + u) * tq, tq) for u in range(unroll)]
        lmins = [tile_fn(r0, False) for r0 in row0s]
        for r0, lmin in zip(row0s, lmins):
            @pl.when(lmin < MIN_DENOMINATOR)
            def _():
                tile_fn(r0, True)
        return carry

    lax.fori_loop(0, n_tiles // unroll, body, 0)


def _proj_diff_kernel(x_ref, g_ref, wqk_ref, wvt_ref, tab_ref, q_ref, k_ref, vt_ref, *, tk):
    tm = x_ref.shape[0]
    h = _rms_rows(x_ref[...], g_ref[...]).astype(BF16)
    for c in range(4):
        y = jnp.dot(h, wqk_ref[:, c * 512:(c + 1) * 512], preferred_element_type=F32)
        dst, base = (q_ref, 0) if c < 2 else (k_ref, 3)
        for j in range(4):
            dst[(c % 2) * 4 + j] = _rope(y[:, j * LANES:(j + 1) * LANES], tab_ref, base,
                                         A_ROT // 2).astype(BF16)
    for half in range(2):
        vt = lax.dot_general(wvt_ref[half * 512:(half + 1) * 512, :], h, NT_DIMS,
                             preferred_element_type=F32)
        for hh in range(4):
            for c in range(tm // tk):
                vt_ref[half * 4 + hh, c] = vt[hh * A_V_DIM:(hh + 1) * A_V_DIM,
                                              c * tk:(c + 1) * tk].astype(BF16)


def _attn_diff_kernel(lam_ref, sub_ref, q_ref, k_ref, vt_ref, o_ref, *, tq, tk, lambda_init):
    low = lax.broadcasted_iota(jnp.int32, (tq, LANES), 1) < A_HEAD_DIM
    sel_low = lax.broadcasted_iota(jnp.int32, (8, LANES), 1) < A_HEAD_DIM
    kmax = [_key_norm_bound(k_ref, slice(None), sel.astype(BF16), tq)
            for sel in (sel_low, jnp.logical_not(sel_low))]
    l1 = jnp.sum(lam_ref[0:1, :] * lam_ref[1:2, :], axis=-1, keepdims=True)
    l2 = jnp.sum(lam_ref[2:3, :] * lam_ref[3:4, :], axis=-1, keepdims=True)
    lam = jnp.exp(l1) - jnp.exp(l2) + lambda_init

    def tile(row0, exact):
        q2 = q_ref[pl.ds(row0, tq), :]
        zero = jnp.zeros_like(q2)
        chains = [(jnp.where(low, q2, zero), slice(None), ()),
                  (jnp.where(low, zero, q2), slice(None), ())]
        shifts = None if exact else [_score_bound(chains[n][0], kmax[n]) for n in range(2)]
        (la, acca), (lb, accb) = _softmax_t(chains, k_ref, vt_ref, tk, shifts)
        o = (acca * (1.0 / la) - lam * (accb * (1.0 / lb))).T
        o_ref[pl.ds(row0, tq), :] = (_rms_rows(o, sub_ref[...]) * (1.0 - lambda_init)).astype(o_ref.dtype)
        return jnp.min(jnp.minimum(la, lb))

    _attend_tiles(tile, q_ref.shape[0] // tq, tq)


def _diff_attention(x, gain, w_qkv, lam_q1, lam_k1, lam_q2, lam_k2, subln, lambda_init, cos, sin):
    b, s, d = x.shape
    tq = _tile(s, 512)
    tk = _tile(s, 512)
    tm = tk
    nck = s // tk
    qk_w = 2 * A_HEADS * A_HEAD_DIM
    scale = A_HEAD_DIM ** -0.5 * LOG2E
    tab = jnp.concatenate([_lane_tables(cos, sin, A_HEAD_DIM, 0, scale),
                           _lane_tables(cos, sin, A_HEAD_DIM, 0, 1.0)])
    wqk = w_qkv[:, :2 * qk_w].astype(BF16)
    wvt = w_qkv[:, 2 * qk_w:].T.astype(BF16)

    head_major = pl.BlockSpec((None, A_HEADS, tm, LANES), lambda bi, i: (bi, 0, i, 0))
    q, k, vt = pl.pallas_call(
        functools.partial(_proj_diff_kernel, tk=tk),
        grid=(b, s // tm),
        in_specs=[
            pl.BlockSpec((None, tm, d), lambda bi, i: (bi, i, 0)),
            _const_spec((1, d)),
            _const_spec((d, 2 * qk_w)),
            _const_spec((A_HEADS * A_V_DIM, d)),
            pl.BlockSpec((6, tm, LANES), lambda bi, i: (0, i, 0)),
        ],
        out_specs=[
            head_major,
            head_major,
            pl.BlockSpec((None, A_HEADS, tm // tk, A_V_DIM, tk), lambda bi, i: (bi, 0, i, 0, 0)),
        ],
        out_shape=[
            jax.ShapeDtypeStruct((b, A_HEADS, s, LANES), BF16),
            jax.ShapeDtypeStruct((b, A_HEADS, s, LANES), BF16),
            jax.ShapeDtypeStruct((b, A_HEADS, nck, A_V_DIM, tk), BF16),
        ],
        compiler_params=_params(("parallel", "parallel")),
        name="proj_diff",
    )(x, gain.reshape(1, d), wqk, wvt, tab)

    lam = jnp.stack([lam_q1, lam_k1, lam_q2, lam_k2]).astype(F32)
    per_head = pl.BlockSpec((None, None, s, LANES), lambda bi, h: (bi, h, 0, 0))
    return pl.pallas_call(
        functools.partial(_attn_diff_kernel, tq=tq, tk=tk, lambda_init=lambda_init),
        grid=(b, A_HEADS),
        in_specs=[
            _const_spec((4, A_HEAD_DIM)),
            _const_spec((1, A_V_DIM)),
            per_head,
            per_head,
            pl.BlockSpec((None, None, nck, A_V_DIM, tk), lambda bi, h: (bi, h, 0, 0, 0)),
        ],
        out_specs=pl.BlockSpec((None, s, A_V_DIM), lambda bi, h: (bi, 0, h)),
        out_shape=jax.ShapeDtypeStruct((b, s, A_HEADS * A_V_DIM), BF16),
        compiler_params=_params(("parallel", "parallel")),
        name="attn_diff",
    )(lam, subln.reshape(1, A_V_DIM).astype(F32), q, k, vt)


def _proj_mla_kernel(x_ref, g_ref, waq_ref, wakv_ref, war_ref, qn_ref, kvn_ref, wqb_ref, wkb_ref,
                     wvt_ref, tab_ref, q_ref, k_ref, vt_ref, *, tk):
    tm = x_ref.shape[0]
    r = B_ROPE // 2
    h = _rms_rows(x_ref[...], g_ref[...]).astype(BF16)
    q_lat = _rms_rows(jnp.dot(h, waq_ref[...], preferred_element_type=F32), qn_ref[...]).astype(BF16)
    kv_lat = _rms_rows(jnp.dot(h, wakv_ref[...], preferred_element_type=F32), kvn_ref[...]).astype(BF16)
    k_rope = _rope(jnp.dot(h, war_ref[...], preferred_element_type=F32), tab_ref, 3, r)
    for c in range(4):
        yq = jnp.dot(q_lat, wqb_ref[:, c * 512:(c + 1) * 512], preferred_element_type=F32)
        yk = jnp.dot(kv_lat, wkb_ref[:, c * 512:(c + 1) * 512], preferred_element_type=F32)
        for j in range(4):
            pair, lanes = 2 * c + j // 2, slice((j % 2) * LANES, (j % 2 + 1) * LANES)
            q_ref[pair, :, lanes] = _rope(yq[:, j * LANES:(j + 1) * LANES], tab_ref, 0, r).astype(BF16)
            k_ref[pair, :, lanes] = (yk[:, j * LANES:(j + 1) * LANES] + k_rope).astype(BF16)
    for half in range(2):
        vt = lax.dot_general(wvt_ref[half * 512:(half + 1) * 512, :], kv_lat, NT_DIMS,
                             preferred_element_type=F32)
        for hh in range(8):
            for c in range(tm // tk):
                vt_ref[half * 8 + hh, c] = vt[hh * B_V:(hh + 1) * B_V, c * tk:(c + 1) * tk].astype(BF16)


def _attn_mla_kernel(q_ref, k_ref, vt_ref, o_ref, *, tq, tk):
    lanes = [slice(e * LANES, (e + 1) * LANES) for e in range(2)]
    ones = jnp.ones((8, LANES), BF16)
    kmax = [_key_norm_bound(k_ref, lanes[e], ones, tq) for e in range(2)]

    def tile(row0, exact):
        chains = [(q_ref[pl.ds(row0, tq), lanes[e]], lanes[e], (e,)) for e in range(2)]
        shifts = None if exact else [_score_bound(chains[n][0], kmax[n]) for n in range(2)]
        (la, acca), (lb, accb) = _softmax_t(chains, k_ref, vt_ref, tk, shifts)
        o = jnp.concatenate([acca * (1.0 / la), accb * (1.0 / lb)], axis=0).T
        o_ref[pl.ds(row0, tq), :] = o.astype(o_ref.dtype)
        return jnp.min(jnp.minimum(la, lb))

    _attend_tiles(tile, q_ref.shape[0] // tq, tq)


def _latent_attention(x, gain, w_a, q_norm, kv_norm, w_qb, w_kvb, cos, sin):
    b, s, d = x.shape
    tq = _tile(s, 512)
    tk = _tile(s, 512)
    tm = tk
    nck = s // tk
    hw = B_HEADS * LANES
    npair = B_HEADS // 2
    scale = (B_NOPE + B_ROPE) ** -0.5 * LOG2E
    tab = jnp.concatenate([_lane_tables(cos, sin, LANES, B_NOPE, scale),
                           _lane_tables(cos, sin, LANES, B_NOPE, 1.0)])
    waq = w_a[:, :B_Q_RANK].astype(BF16)
    wakv = w_a[:, B_Q_RANK:B_Q_RANK + B_KV_RANK].astype(BF16)
    war = jnp.zeros((d, LANES), F32).at[:, B_NOPE:B_NOPE + B_ROPE].set(
        w_a[:, B_Q_RANK + B_KV_RANK:]).astype(BF16)
    pad = LANES - (B_NOPE + B_ROPE)
    wqb = jnp.pad(w_qb.reshape(B_Q_RANK, B_HEADS, B_NOPE + B_ROPE),
                  ((0, 0), (0, 0), (0, pad))).reshape(B_Q_RANK, hw).astype(BF16)
    wkv3 = w_kvb.reshape(B_KV_RANK, B_HEADS, B_NOPE + B_V)
    wkb = jnp.pad(wkv3[:, :, :B_NOPE], ((0, 0), (0, 0), (0, LANES - B_NOPE))
                  ).reshape(B_KV_RANK, hw).astype(BF16)
    wvt = wkv3[:, :, B_NOPE:].reshape(B_KV_RANK, B_HEADS * B_V).T.astype(BF16)

    pair_major = pl.BlockSpec((None, npair, tm, 2 * LANES), lambda bi, i: (bi, 0, i, 0))
    q, k, vt = pl.pallas_call(
        functools.partial(_proj_mla_kernel, tk=tk),
        grid=(b, s // tm),
        in_specs=[
            pl.BlockSpec((None, tm, d), lambda bi, i: (bi, i, 0)),
            _const_spec((1, d)),
            _const_spec((d, B_Q_RANK)),
            _const_spec((d, B_KV_RANK)),
            _const_spec((d, LANES)),
            _const_spec((1, B_Q_RANK)),
            _const_spec((1, B_KV_RANK)),
            _const_spec((B_Q_RANK, hw)),
            _const_spec((B_KV_RANK, hw)),
            _const_spec((B_HEADS * B_V, B_KV_RANK)),
            pl.BlockSpec((6, tm, LANES), lambda bi, i: (0, i, 0)),
        ],
        out_specs=[
            pair_major,
            pair_major,
            pl.BlockSpec((None, B_HEADS, tm // tk, B_V, tk), lambda bi, i: (bi, 0, i, 0, 0)),
        ],
        out_shape=[
            jax.ShapeDtypeStruct((b, npair, s, 2 * LANES), BF16),
            jax.ShapeDtypeStruct((b, npair, s, 2 * LANES), BF16),
            jax.ShapeDtypeStruct((b, B_HEADS, nck, B_V, tk), BF16),
        ],
        compiler_params=_params(("parallel", "parallel")),
        name="proj_mla",
    )(x, gain.reshape(1, d), waq, wakv, war, q_norm.reshape(1, -1), kv_norm.reshape(1, -1),
      wqb, wkb, wvt, tab)

    per_pair = pl.BlockSpec((None, None, s, 2 * LANES), lambda bi, h: (bi, h, 0, 0))
    return pl.pallas_call(
        functools.partial(_attn_mla_kernel, tq=tq, tk=tk),
        grid=(b, npair),
        in_specs=[
            per_pair,
            per_pair,
            pl.BlockSpec((None, 2, nck, B_V, tk), lambda bi, h: (bi, h, 0, 0, 0)),
        ],
        out_specs=pl.BlockSpec((None, s, 2 * B_V), lambda bi, h: (bi, 0, h)),
        out_shape=jax.ShapeDtypeStruct((b, s, B_HEADS * B_V), BF16),
        compiler_params=_params(("parallel", "parallel")),
        name="attn_mla",
    )(q, k, vt)


def _proj_dil_kernel(x_ref, g_ref, w_ref, tab_ref, *refs):
    outs, scr = refs[:-1], refs[-1]
    tl = x_ref.shape[0]
    h = _rms_rows(x_ref[...], g_ref[...]).astype(BF16)
    width = C_HEADS * C_HEAD_DIM
    for g, (_, dil) in enumerate(C_GROUPS):
        for c in range(3 * width // 512):
            col = g * 3 * width + c * 512
            y = jnp.dot(h, w_ref[:, col:col + 512], preferred_element_type=F32)
            which = c // 2
            dst = outs[3 * g + which]
            lo = (c % 2) * 512
            for j in range(4):
                yj = y[:, j * LANES:(j + 1) * LANES]
                if which < 2:
                    yj = _rope(yj, tab_ref, 3 * which, C_ROT // 2)
                cols = slice(lo + j * LANES, lo + (j + 1) * LANES)
                if dil == 1:
                    dst[0, :, cols] = yj.astype(BF16)
                    continue
                scr[j] = yj
                for p in range(dil):
                    dst[p, :, cols] = scr[j, pl.ds(p, tl // dil, stride=dil), :].astype(BF16)


def _attn_dil_kernel(q_ref, kp_ref, kc_ref, kn_ref, vp_ref, vc_ref, vn_ref, o_ref, lse_ref, *, seq, tq):
    tt = q_ref.shape[0]
    nsub = tt // tq
    w = tq + 2 * C_HALO
    t0 = pl.program_id(2) * tt
    col = lax.broadcasted_iota(jnp.int32, (tq, w), 1)
    band = jnp.abs(col - C_HALO - lax.broadcasted_iota(jnp.int32, (tq, w), 0)) <= C_HALO
    head_lane = lax.broadcasted_iota(jnp.int32, (tq, LANES), 1)
    low = head_lane < C_HEAD_DIM
    cache = {}

    def window(prev_ref, cur_ref, next_ref, sub, lanes):
        lo, hi = sub * tq - C_HALO, (sub + 1) * tq + C_HALO
        parts = []
        if lo < 0:
            parts.append(prev_ref[:, lanes])
        parts.append(cur_ref[max(lo, 0):min(hi, tt), lanes])
        if hi > tt:
            parts.append(next_ref[:, lanes])
        return parts[0] if len(parts) == 1 else jnp.concatenate(parts, axis=0)

    def operands(sub, j):
        if (sub, j) not in cache:
            lanes = slice(j * LANES, (j + 1) * LANES)
            cache[(sub, j)] = (q_ref[sub * tq:(sub + 1) * tq, lanes],
                               window(kp_ref, kc_ref, kn_ref, sub, lanes),
                               window(vp_ref, vc_ref, vn_ref, sub, lanes))
        return cache[(sub, j)]

    def mask(sub):
        if ("mask", sub) not in cache:
            kpos = t0 + sub * tq - C_HALO + col
            cache[("mask", sub)] = band & (kpos >= 0) & (kpos < seq)
        return cache[("mask", sub)]

    def scores(unit):
        sub, j, e = unit
        q2 = operands(sub, j)[0]
        zero = jnp.zeros_like(q2)
        qm = jnp.where(low, q2, zero) if e == 0 else jnp.where(low, zero, q2)
        return lax.dot_general(qm, operands(sub, j)[1], NT_DIMS, preferred_element_type=F32)

    units = [(sub, j, e) for sub in range(nsub) for j in range(C_HEADS // 2) for e in range(2)]
    sc = scores(units[0])
    halves, lse_all = [], None
    for i, (sub, j, e) in enumerate(units):
        sc_next = scores(units[i + 1]) if i + 1 < len(units) else None
        sc = jnp.where(mask(sub), sc, NEG_BIG)
        m = jnp.max(sc, axis=-1, keepdims=True)
        p = jnp.exp2(sc - m)
        l = jnp.sum(p, axis=-1, keepdims=True)
        pv = jnp.dot(p.astype(BF16), operands(sub, j)[2], preferred_element_type=F32)
        halves.append(pv * (1.0 / l))
        lse = m + jnp.log2(l)
        lse_all = lse + jnp.zeros((tq, LANES), F32) if (j, e) == (0, 0) else jnp.where(
            head_lane == 2 * j + e, lse, lse_all)
        if e == 1:
            rows = slice(sub * tq, (sub + 1) * tq)
            o_ref[rows, j * LANES:(j + 1) * LANES] = jnp.where(low, halves[0], halves[1]).astype(o_ref.dtype)
            halves = []
            if j == C_HEADS // 2 - 1:
                lse_ref[rows, :] = lse_all
        sc = sc_next


def _combine_dil_kernel(o1_ref, o2_ref, o3_ref, l1_ref, l2_ref, l3_ref, o_ref, so_ref, sl_ref):
    rows = o_ref.shape[0]
    nlg = o_ref.shape[1] // LANES
    os_, ls = [], []
    for g, ((_, dil), oref, lref) in enumerate(zip(C_GROUPS, (o1_ref, o2_ref, o3_ref),
                                                   (l1_ref, l2_ref, l3_ref))):
        if dil == 1:
            os_.append(oref[0].astype(F32))
            ls.append(lref[0])
            continue
        for p in range(dil):
            og = oref[p].astype(F32)
            for j in range(nlg):
                so_ref[g - 1, j, pl.ds(p, rows // dil, stride=dil), :] = og[:, j * LANES:(j + 1) * LANES]
            sl_ref[g - 1, pl.ds(p, rows // dil, stride=dil), :] = lref[p]
        os_.append(jnp.concatenate([so_ref[g - 1, j] for j in range(nlg)], axis=-1))
        ls.append(sl_ref[g - 1])
    m = jnp.maximum(jnp.maximum(ls[0], ls[1]), ls[2])
    es = [jnp.exp2(l - m) for l in ls]
    inv = 1.0 / (es[0] + es[1] + es[2])
    width = C_HEADS * C_HEAD_DIM
    expand = (lax.broadcasted_iota(jnp.int32, (LANES, width), 1) // C_HEAD_DIM
              == lax.broadcasted_iota(jnp.int32, (LANES, width), 0)).astype(BF16)
    acc = jnp.zeros(o_ref.shape, F32)
    for e, og in zip(es, os_):
        wt = e * inv
        hi = wt.astype(BF16)
        lo = (wt - hi.astype(F32)).astype(BF16)
        wide = (jnp.dot(hi, expand, preferred_element_type=F32)
                + jnp.dot(lo, expand, preferred_element_type=F32))
        acc = acc + wide * og
    o_ref[...] = acc.astype(o_ref.dtype)


def _dilated_attention(x, gain, w_qkv, cos, sin):
    b, s, d = x.shape
    width = C_HEADS * C_HEAD_DIM
    scale = C_HEAD_DIM ** -0.5 * LOG2E
    tab = jnp.concatenate([_lane_tables(cos, sin, C_HEAD_DIM, 0, scale),
                           _lane_tables(cos, sin, C_HEAD_DIM, 0, 1.0)])
    dmax = max(dil for _, dil in C_GROUPS)
    for window, dil in C_GROUPS:
        assert (window // 2) // dil == C_HALO and (s // dil) % C_HALO == 0

    tl = _tile(s, 256)
    assert tl % (BF16_SUBLANES * dmax) == 0
    phase_spec = lambda dil, cols: pl.BlockSpec((None, dil, tl // dil, cols),
                                                lambda bi, i: (bi, 0, i, 0))
    qkv = pl.pallas_call(
        _proj_dil_kernel,
        grid=(b, s // tl),
        in_specs=[
            pl.BlockSpec((None, tl, d), lambda bi, i: (bi, i, 0)),
            _const_spec((1, d)),
            _const_spec((d, len(C_GROUPS) * 3 * width)),
            pl.BlockSpec((6, tl, LANES), lambda bi, i: (0, i, 0)),
        ],
        out_specs=[phase_spec(dil, width) for _, dil in C_GROUPS for _ in range(3)],
        out_shape=[jax.ShapeDtypeStruct((b, dil, s // dil, width), BF16)
                   for _, dil in C_GROUPS for _ in range(3)],
        scratch_shapes=[pltpu.VMEM((4, tl, LANES), F32)],
        compiler_params=_params(("parallel", "parallel")),
        name="proj_dil",
    )(x, gain.reshape(1, d), w_qkv.astype(BF16), tab)

    outs, lses = [], []
    for g, (_, dil) in enumerate(C_GROUPS):
        q, k, v = qkv[3 * g:3 * g + 3]
        seq = s // dil
        tt = _tile(seq, 512)
        tq = _tile(tt, 128)
        r = tt // C_HALO
        nhalo = seq // C_HALO
        cur = pl.BlockSpec((None, None, tt, width), lambda bi, p, i: (bi, p, i, 0))
        prev = pl.BlockSpec((None, None, C_HALO, width),
                            lambda bi, p, i, r=r: (bi, p, jnp.maximum(i * r - 1, 0), 0))
        nxt = pl.BlockSpec((None, None, C_HALO, width),
                           lambda bi, p, i, r=r, nhalo=nhalo: (bi, p, jnp.minimum((i + 1) * r, nhalo - 1), 0))
        o_g, lse_g = pl.pallas_call(
            functools.partial(_attn_dil_kernel, seq=seq, tq=tq),
            grid=(b, dil, seq // tt),
            in_specs=[cur, prev, cur, nxt, prev, cur, nxt],
            out_specs=[cur, pl.BlockSpec((None, None, tt, LANES), lambda bi, p, i: (bi, p, i, 0))],
            out_shape=[jax.ShapeDtypeStruct((b, dil, seq, width), BF16),
                       jax.ShapeDtypeStruct((b, dil, seq, LANES), F32)],
            compiler_params=_params(("parallel", "parallel", "parallel")),
            name=f"attn_dil{g}",
        )(q, k, k, k, v, v, v)
        outs.append(o_g)
        lses.append(lse_g)

    return pl.pallas_call(
        _combine_dil_kernel,
        grid=(b, s // tl),
        in_specs=([phase_spec(dil, width) for _, dil in C_GROUPS]
                  + [phase_spec(dil, LANES) for _, dil in C_GROUPS]),
        out_specs=pl.BlockSpec((None, tl, width), lambda bi, i: (bi, i, 0)),
        out_shape=jax.ShapeDtypeStruct((b, s, width), BF16),
        scratch_shapes=[pltpu.VMEM((len(C_GROUPS) - 1, width // LANES, tl, LANES), F32),
                        pltpu.VMEM((len(C_GROUPS) - 1, tl, LANES), F32)],
        compiler_params=_params(("parallel", "parallel")),
        name="combine_dil",
    )(*outs, *lses)


def _post_kernel(o_ref, x_ref, wo_ref, g_ref, wgu_ref, wout_ref, fg_ref, y_ref, acc_ref, *,
                 apply_final):
    x1 = x_ref[...] + jnp.dot(o_ref[...], wo_ref[...], preferred_element_type=F32)
    h = _rms_rows(x1, g_ref[...]).astype(BF16)
    acc_ref[...] = x1
    for c in range(FFN_DIM // FFN_CHUNK):
        lo = c * FFN_CHUNK
        gate = jnp.dot(h, wgu_ref[:, lo:lo + FFN_CHUNK], preferred_element_type=F32)
        up = jnp.dot(h, wgu_ref[:, FFN_DIM + lo:FFN_DIM + lo + FFN_CHUNK], preferred_element_type=F32)
        act = (gate * jax.nn.sigmoid(gate) * up).astype(BF16)
        acc_ref[...] += jnp.dot(act, wout_ref[lo:lo + FFN_CHUNK, :], preferred_element_type=F32)
    y = acc_ref[...]
    if apply_final:
        y = _rms_rows(y, fg_ref[...])
    y_ref[...] = y


def _post(o, x, w_o, ffn_gain, w_gu, w_out, final_gain, apply_final):
    b, s, d = x.shape
    m = b * s
    tm = _tile(m, 512)
    row = lambda i: (i, 0)
    y = pl.pallas_call(
        functools.partial(_post_kernel, apply_final=apply_final),
        grid=(m // tm,),
        in_specs=[
            pl.BlockSpec((tm, d), row),
            pl.BlockSpec((tm, d), row),
            _const_spec((d, d)),
            _const_spec((1, d)),
            _const_spec((d, 2 * FFN_DIM)),
            _const_spec((FFN_DIM, d)),
            _const_spec((1, d)),
        ],
        out_specs=pl.BlockSpec((tm, d), row),
        out_shape=jax.ShapeDtypeStruct((m, d), F32),
        scratch_shapes=[pltpu.VMEM((tm, d), F32)],
        compiler_params=_params(("parallel",)),
        name="post",
    )(o.reshape(m, d), x.reshape(m, d), w_o.astype(BF16), ffn_gain.reshape(1, d),
      w_gu.astype(BF16), w_out.astype(BF16), final_gain.reshape(1, d))
    return y.reshape(b, s, d)


def kernel(x, attn_norm, ffn_norm, final_norm, a_w_qkv, a_lambda_q1, a_lambda_k1, a_lambda_q2,
           a_lambda_k2, a_subln, a_w_o, b_w_a, b_q_norm, b_kv_norm, b_w_qb, b_w_kvb, b_w_o,
           c_w_qkv, c_w_o, f_w_gu, f_w_out):
    depth = attn_norm.shape[0]
    s = x.shape[1]
    cos_p, sin_p = _rope_tables(s, A_ROT)
    cos_b, sin_b = _rope_tables(s, B_ROPE)
    for i in range(depth):
        mixer, j = i % N_MIXERS, i // N_MIXERS
        if mixer == 0:
            lambda_init = 0.8 - 0.6 * math.exp(-0.3 * i)
            o = _diff_attention(x, attn_norm[i], a_w_qkv[j], a_lambda_q1[j], a_lambda_k1[j],
                                a_lambda_q2[j], a_lambda_k2[j], a_subln[j], lambda_init, cos_p, sin_p)
            w_o = a_w_o[j]
        elif mixer == 1:
            o = _latent_attention(x, attn_norm[i], b_w_a[j], b_q_norm[j], b_kv_norm[j], b_w_qb[j],
                                  b_w_kvb[j], cos_b, sin_b)
            w_o = b_w_o[j]
        else:
            o = _dilated_attention(x, attn_norm[i], c_w_qkv[j], cos_p, sin_p)
            w_o = c_w_o[j]
        x = _post(o, x, w_o, ffn_norm[i], f_w_gu[i], f_w_out[i], final_norm, i == depth - 1)
    return x
```

```python
import functools
import math

import jax
import jax.numpy as jnp
from jax import lax
from jax.experimental import pallas as pl
from jax.experimental.pallas import tpu as pltpu

F32 = jnp.float32
BF16 = jnp.bfloat16

D_MODEL = 1024
N_MIXERS = 3
ROPE_THETA = 500000.0
EPS = 1e-6
NEG_BIG = -1e30
LOG2E = 1.4426950408889634

A_HEADS = 8
A_HEAD_DIM = 64
A_V_DIM = 128
A_ROT = 16

B_HEADS = 16
B_Q_RANK = 384
B_KV_RANK = 256
B_NOPE = 64
B_ROPE = 32
B_V = 64

C_GROUPS = ((128, 1), (512, 4), (2048, 16))
C_HEADS = 16
C_HEAD_DIM = 64
C_ROT = 16
C_HALO = 64

FFN_DIM = 2816
FFN_CHUNK = 256

LANES = 128
BF16_SUBLANES = 16
Q_UNROLL = 2
VMEM_LIMIT_BYTES = 56 * 1024 * 1024

NT_DIMS = (((1,), (1,)), ((), ()))
BOUND_MARGIN = 1.01
MIN_DENOMINATOR = 2.0 ** -60


def _tile(n, pref):
    t = min(n, pref)
    assert n % t == 0, (n, t)
    return t


def _params(sem):
    return pltpu.CompilerParams(dimension_semantics=sem, vmem_limit_bytes=VMEM_LIMIT_BYTES)


def _const_spec(shape):
    nd = len(shape)
    return pl.BlockSpec(shape, lambda *_: (0,) * nd, pipeline_mode=pl.Buffered(1))


def _rms_rows(x, gain):
    return x * lax.rsqrt(jnp.mean(x * x, axis=-1, keepdims=True) + EPS) * gain


def _rope_tables(seq_len, rot_dim):
    pos = jnp.arange(seq_len, dtype=F32)
    inv = ROPE_THETA ** (-jnp.arange(0, rot_dim, 2, dtype=F32) / rot_dim)
    ang = pos[:, None] * inv[None, :]
    return jnp.cos(ang), jnp.sin(ang)


def _lane_tables(cos, sin, period, offset, scale):
    s, r = cos.shape
    c = jnp.ones((s, period), F32)
    c = c.at[:, offset:offset + r].set(cos).at[:, offset + r:offset + 2 * r].set(cos)
    sa = jnp.zeros((s, period), F32).at[:, offset:offset + r].set(-sin)
    sb = jnp.zeros((s, period), F32).at[:, offset + r:offset + 2 * r].set(sin)
    rep = LANES // period
    return jnp.stack([jnp.tile(t, (1, rep)) for t in (c, sa, sb)]) * scale


def _rope(x, tab_ref, base, r):
    return (x * tab_ref[base]
            + pltpu.roll(x, LANES - r, 1) * tab_ref[base + 1]
            + pltpu.roll(x, r, 1) * tab_ref[base + 2])


def _softmax_t(chains, k_ref, vt_ref, tk, shifts=None):
    rows = vt_ref.shape[-2]
    nchunk = vt_ref.shape[-3]
    state = [(jnp.full((1, qm.shape[0]), NEG_BIG, F32), jnp.zeros((1, qm.shape[0]), F32),
              jnp.zeros((rows, qm.shape[0]), F32)) for qm, _, _ in chains]
    items = [(c, n) for c in range(nchunk) for n in range(len(chains))]

    def scores(item):
        c, n = item
        qm, k_lanes, _ = chains[n]
        return lax.dot_general(k_ref[c * tk:(c + 1) * tk, k_lanes], qm, NT_DIMS,
                               preferred_element_type=F32)

    st = scores(items[0])
    for i, (c, n) in enumerate(items):
        st_next = scores(items[i + 1]) if i + 1 < len(items) else None
        m, l, acc = state[n]
        if shifts is None:
            m_new = jnp.maximum(m, jnp.max(st, axis=0, keepdims=True))
            alpha = jnp.exp2(m - m_new)
            p = jnp.exp2(st - m_new)
            l, acc = l * alpha, acc * alpha
        else:
            m_new = m
            p = jnp.exp2(st - shifts[n])
        l = l + jnp.sum(p, axis=0, keepdims=True)
        acc = acc + jnp.dot(vt_ref[chains[n][2] + (c,)], p.astype(BF16), preferred_element_type=F32)
        state[n] = (m_new, l, acc)
        st = st_next
    return [(l, acc) for _, l, acc in state]


def _key_norm_bound(k_ref, k_lanes, sel, width):
    kk = k_ref[:, k_lanes]
    kn2 = lax.dot_general(sel, kk * kk, NT_DIMS, preferred_element_type=F32)
    return jnp.broadcast_to(jnp.sqrt(jnp.max(kn2, axis=-1, keepdims=True)), (8, width))


def _score_bound(qm, kmax):
    ones = jnp.ones((8, qm.shape[1]), BF16)
    qn2 = lax.dot_general(ones, qm * qm, NT_DIMS, preferred_element_type=F32)
    return jnp.sqrt(qn2[0:1]) * kmax[0:1] * BOUND_MARGIN


def _attend_tiles(tile_fn, n_tiles, tq):
    unroll = Q_UNROLL if n_tiles % Q_UNROLL == 0 else 1

    def body(it, carry):
        row0s = [pl.multiple_of((it * unroll + u) * tq, tq) for u in range(unroll)]
        lmins = [tile_fn(r0, False) for r0 in row0s]
        for r0, lmin in zip(row0s, lmins):
            @pl.when(lmin < MIN_DENOMINATOR)
            def _():
                tile_fn(r0, True)
        return carry

    lax.fori_loop(0, n_tiles // unroll, body, 0)


def _proj_diff_kernel(x_ref, g_ref, wqk_ref, wvt_ref, tab_ref, q_ref, k_ref, vt_ref, *, tk):
    tm = x_ref.shape[0]
    h = _rms_rows(x_ref[...], g_ref[...]).astype(BF16)
    for c in range(4):
        y = jnp.dot(h, wqk_ref[:, c * 512:(c + 1) * 512], preferred_element_type=F32)
        dst, base = (q_ref, 0) if c < 2 else (k_ref, 3)
        for j in range(4):
            dst[(c % 2) * 4 + j] = _rope(y[:, j * LANES:(j + 1) * LANES], tab_ref, base,
                                         A_ROT // 2).astype(BF16)
    for half in range(2):
        vt = lax.dot_general(wvt_ref[half * 512:(half + 1) * 512, :], h, NT_DIMS,
                             preferred_element_type=F32)
        for hh in range(4):
            for c in range(tm // tk):
                vt_ref[half * 4 + hh, c] = vt[hh * A_V_DIM:(hh + 1) * A_V_DIM,
                                              c * tk:(c + 1) * tk].astype(BF16)


def _attn_diff_kernel(lam_ref, sub_ref, q_ref, k_ref, vt_ref, o_ref, *, tq, tk, lambda_init):
    low = lax.broadcasted_iota(jnp.int32, (tq, LANES), 1) < A_HEAD_DIM
    sel_low = lax.broadcasted_iota(jnp.int32, (8, LANES), 1) < A_HEAD_DIM
    kmax = [_key_norm_bound(k_ref, slice(None), sel.astype(BF16), tq)
            for sel in (sel_low, jnp.logical_not(sel_low))]
    l1 = jnp.sum(lam_ref[0:1, :] * lam_ref[1:2, :], axis=-1, keepdims=True)
    l2 = jnp.sum(lam_ref[2:3, :] * lam_ref[3:4, :], axis=-1, keepdims=True)
    lam = jnp.exp(l1) - jnp.exp(l2) + lambda_init

    def tile(row0, exact):
        q2 = q_ref[pl.ds(row0, tq), :]
        zero = jnp.zeros_like(q2)
        chains = [(jnp.where(low, q2, zero), slice(None), ()),
                  (jnp.where(low, zero, q2), slice(None), ())]
        shifts = None if exact else [_score_bound(chains[n][0], kmax[n]) for n in range(2)]
        (la, acca), (lb, accb) = _softmax_t(chains, k_ref, vt_ref, tk, shifts)
        o = (acca * (1.0 / la) - lam * (accb * (1.0 / lb))).T
        o_ref[pl.ds(row0, tq), :] = (_rms_rows(o, sub_ref[...]) * (1.0 - lambda_init)).astype(o_ref.dtype)
        return jnp.min(jnp.minimum(la, lb))

    _attend_tiles(tile, q_ref.shape[0] // tq, tq)


def _diff_attention(x, gain, w_qkv, lam_q1, lam_k1, lam_q2, lam_k2, subln, lambda_init, cos, sin):
    b, s, d = x.shape
    tq = _tile(s, 512)
    tk = _tile(s, 512)
    tm = tk
    nck = s // tk
    qk_w = 2 * A_HEADS * A_HEAD_DIM
    scale = A_HEAD_DIM ** -0.5 * LOG2E
    tab = jnp.concatenate([_lane_tables(cos, sin, A_HEAD_DIM, 0, scale),
                           _lane_tables(cos, sin, A_HEAD_DIM, 0, 1.0)])
    wqk = w_qkv[:, :2 * qk_w].astype(BF16)
    wvt = w_qkv[:, 2 * qk_w:].T.astype(BF16)

    head_major = pl.BlockSpec((None, A_HEADS, tm, LANES), lambda bi, i: (bi, 0, i, 0))
    q, k, vt = pl.pallas_call(
        functools.partial(_proj_diff_kernel, tk=tk),
        grid=(b, s // tm),
        in_specs=[
            pl.BlockSpec((None, tm, d), lambda bi, i: (bi, i, 0)),
            _const_spec((1, d)),
            _const_spec((d, 2 * qk_w)),
            _const_spec((A_HEADS * A_V_DIM, d)),
            pl.BlockSpec((6, tm, LANES), lambda bi, i: (0, i, 0)),
        ],
        out_specs=[
            head_major,
            head_major,
            pl.BlockSpec((None, A_HEADS, tm // tk, A_V_DIM, tk), lambda bi, i: (bi, 0, i, 0, 0)),
        ],
        out_shape=[
            jax.ShapeDtypeStruct((b, A_HEADS, s, LANES), BF16),
            jax.ShapeDtypeStruct((b, A_HEADS, s, LANES), BF16),
            jax.ShapeDtypeStruct((b, A_HEADS, nck, A_V_DIM, tk), BF16),
        ],
        compiler_params=_params(("parallel", "parallel")),
        name="proj_diff",
    )(x, gain.reshape(1, d), wqk, wvt, tab)

    lam = jnp.stack([lam_q1, lam_k1, lam_q2, lam_k2]).astype(F32)
    per_head = pl.BlockSpec((None, None, s, LANES), lambda bi, h: (bi, h, 0, 0))
    return pl.pallas_call(
        functools.partial(_attn_diff_kernel, tq=tq, tk=tk, lambda_init=lambda_init),
        grid=(b, A_HEADS),
        in_specs=[
            _const_spec((4, A_HEAD_DIM)),
            _const_spec((1, A_V_DIM)),
            per_head,
            per_head,
            pl.BlockSpec((None, None, nck, A_V_DIM, tk), lambda bi, h: (bi, h, 0, 0, 0)),
        ],
        out_specs=pl.BlockSpec((None, s, A_V_DIM), lambda bi, h: (bi, 0, h)),
        out_shape=jax.ShapeDtypeStruct((b, s, A_HEADS * A_V_DIM), BF16),
        compiler_params=_params(("parallel", "parallel")),
        name="attn_diff",
    )(lam, subln.reshape(1, A_V_DIM).astype(F32), q, k, vt)


def _proj_mla_kernel(x_ref, g_ref, waq_ref, wakv_ref, war_ref, qn_ref, kvn_ref, wqb_ref, wkb_ref,
                     wvt_ref, tab_ref, q_ref, k_ref, vt_ref, *, tk):
    tm = x_ref.shape[0]
    r = B_ROPE // 2
    h = _rms_rows(x_ref[...], g_ref[...]).astype(BF16)
    q_lat = _rms_rows(jnp.dot(h, waq_ref[...], preferred_element_type=F32), qn_ref[...]).astype(BF16)
    kv_lat = _rms_rows(jnp.dot(h, wakv_ref[...], preferred_element_type=F32), kvn_ref[...]).astype(BF16)
    k_rope = _rope(jnp.dot(h, war_ref[...], preferred_element_type=F32), tab_ref, 3, r)
    for c in range(4):
        yq = jnp.dot(q_lat, wqb_ref[:, c * 512:(c + 1) * 512], preferred_element_type=F32)
        yk = jnp.dot(kv_lat, wkb_ref[:, c * 512:(c + 1) * 512], preferred_element_type=F32)
        for j in range(4):
            pair, lanes = 2 * c + j // 2, slice((j % 2) * LANES, (j % 2 + 1) * LANES)
            q_ref[pair, :, lanes] = _rope(yq[:, j * LANES:(j + 1) * LANES], tab_ref, 0, r).astype(BF16)
            k_ref[pair, :, lanes] = (yk[:, j * LANES:(j + 1) * LANES] + k_rope).astype(BF16)
    for half in range(2):
        vt = lax.dot_general(wvt_ref[half * 512:(half + 1) * 512, :], kv_lat, NT_DIMS,
                             preferred_element_type=F32)
        for hp in range(4):
            for c in range(tm // tk):
                vt_ref[half * 4 + hp, c] = vt[hp * LANES:(hp + 1) * LANES, c * tk:(c + 1) * tk].astype(BF16)


def _attn_mla_kernel(q_ref, k_ref, vt_ref, o_ref, *, tq, tk):
    lanes = [slice(e * LANES, (e + 1) * LANES) for e in range(2)]
    ones = jnp.ones((8, LANES), BF16)
    kmax = [_key_norm_bound(k_ref, lanes[e], ones, tq) for e in range(2)]

    def tile(row0, exact):
        chains = [(q_ref[pl.ds(row0, tq), lanes[e]], lanes[e], ()) for e in range(2)]
        shifts = None if exact else [_score_bound(chains[n][0], kmax[n]) for n in range(2)]
        (la, acca), (lb, accb) = _softmax_t(chains, k_ref, vt_ref, tk, shifts)
        o = jnp.concatenate([acca[0:B_V] * (1.0 / la), accb[B_V:2 * B_V] * (1.0 / lb)], axis=0).T
        o_ref[pl.ds(row0, tq), :] = o.astype(o_ref.dtype)
        return jnp.min(jnp.minimum(la, lb))

    _attend_tiles(tile, q_ref.shape[0] // tq, tq)


def _latent_attention(x, gain, w_a, q_norm, kv_norm, w_qb, w_kvb, cos, sin):
    b, s, d = x.shape
    tq = _tile(s, 512)
    tk = _tile(s, 512)
    tm = tk
    nck = s // tk
    hw = B_HEADS * LANES
    npair = B_HEADS // 2
    scale = (B_NOPE + B_ROPE) ** -0.5 * LOG2E
    tab = jnp.concatenate([_lane_tables(cos, sin, LANES, B_NOPE, scale),
                           _lane_tables(cos, sin, LANES, B_NOPE, 1.0)])
    waq = w_a[:, :B_Q_RANK].astype(BF16)
    wakv = w_a[:, B_Q_RANK:B_Q_RANK + B_KV_RANK].astype(BF16)
    war = jnp.zeros((d, LANES), F32).at[:, B_NOPE:B_NOPE + B_ROPE].set(
        w_a[:, B_Q_RANK + B_KV_RANK:]).astype(BF16)
    pad = LANES - (B_NOPE + B_ROPE)
    wqb = jnp.pad(w_qb.reshape(B_Q_RANK, B_HEADS, B_NOPE + B_ROPE),
                  ((0, 0), (0, 0), (0, pad))).reshape(B_Q_RANK, hw).astype(BF16)
    wkv3 = w_kvb.reshape(B_KV_RANK, B_HEADS, B_NOPE + B_V)
    wkb = jnp.pad(wkv3[:, :, :B_NOPE], ((0, 0), (0, 0), (0, LANES - B_NOPE))
                  ).reshape(B_KV_RANK, hw).astype(BF16)
    wvt = wkv3[:, :, B_NOPE:].reshape(B_KV_RANK, B_HEADS * B_V).T.astype(BF16)

    pair_major = pl.BlockSpec((None, npair, tm, 2 * LANES), lambda bi, i: (bi, 0, i, 0))
    q, k, vt = pl.pallas_call(
        functools.partial(_proj_mla_kernel, tk=tk),
        grid=(b, s // tm),
        in_specs=[
            pl.BlockSpec((None, tm, d), lambda bi, i: (bi, i, 0)),
            _const_spec((1, d)),
            _const_spec((d, B_Q_RANK)),
            _const_spec((d, B_KV_RANK)),
            _const_spec((d, LANES)),
            _const_spec((1, B_Q_RANK)),
            _const_spec((1, B_KV_RANK)),
            _const_spec((B_Q_RANK, hw)),
            _const_spec((B_KV_RANK, hw)),
            _const_spec((B_HEADS * B_V, B_KV_RANK)),
            pl.BlockSpec((6, tm, LANES), lambda bi, i: (0, i, 0)),
        ],
        out_specs=[
            pair_major,
            pair_major,
            pl.BlockSpec((None, npair, tm // tk, 2 * B_V, tk), lambda bi, i: (bi, 0, i, 0, 0)),
        ],
        out_shape=[
            jax.ShapeDtypeStruct((b, npair, s, 2 * LANES), BF16),
            jax.ShapeDtypeStruct((b, npair, s, 2 * LANES), BF16),
            jax.ShapeDtypeStruct((b, npair, nck, 2 * B_V, tk), BF16),
        ],
        compiler_params=_params(("parallel", "parallel")),
        name="proj_mla",
    )(x, gain.reshape(1, d), waq, wakv, war, q_norm.reshape(1, -1), kv_norm.reshape(1, -1),
      wqb, wkb, wvt, tab)

    per_pair = pl.BlockSpec((None, None, s, 2 * LANES), lambda bi, h: (bi, h, 0, 0))
    return pl.pallas_call(
        functools.partial(_attn_mla_kernel, tq=tq, tk=tk),
        grid=(b, npair),
        in_specs=[
            per_pair,
            per_pair,
            pl.BlockSpec((None, None, nck, 2 * B_V, tk), lambda bi, h: (bi, h, 0, 0, 0)),
        ],
        out_specs=pl.BlockSpec((None, s, 2 * B_V), lambda bi, h: (bi, 0, h)),
        out_shape=jax.ShapeDtypeStruct((b, s, B_HEADS * B_V), BF16),
        compiler_params=_params(("parallel", "parallel")),
        name="attn_mla",
    )(q, k, vt)


def _proj_dil_kernel(x_ref, g_ref, w_ref, tab_ref, *refs):
    outs, scr = refs[:-1], refs[-1]
    tl = x_ref.shape[0]
    h = _rms_rows(x_ref[...], g_ref[...]).astype(BF16)
    width = C_HEADS * C_HEAD_DIM
    for g, (_, dil) in enumerate(C_GROUPS):
        for c in range(3 * width // 512):
            col = g * 3 * width + c * 512
            y = jnp.dot(h, w_ref[:, col:col + 512], preferred_element_type=F32)
            which = c // 2
            dst = outs[3 * g + which]
            lo = (c % 2) * 512
            for j in range(4):
                yj = y[:, j * LANES:(j + 1) * LANES]
                if which < 2:
                    yj = _rope(yj, tab_ref, 3 * which, C_ROT // 2)
                cols = slice(lo + j * LANES, lo + (j + 1) * LANES)
                if dil == 1:
                    dst[0, :, cols] = yj.astype(BF16)
                    continue
                scr[j] = yj
                for p in range(dil):
                    dst[p, :, cols] = scr[j, pl.ds(p, tl // dil, stride=dil), :].astype(BF16)


def _attn_dil_kernel(q_ref, kp_ref, kc_ref, kn_ref, vp_ref, vc_ref, vn_ref, o_ref, lse_ref, *, seq, tq):
    tt = q_ref.shape[0]
    nsub = tt // tq
    w = tq + 2 * C_HALO
    t0 = pl.program_id(2) * tt
    col = lax.broadcasted_iota(jnp.int32, (tq, w), 1)
    band = jnp.abs(col - C_HALO - lax.broadcasted_iota(jnp.int32, (tq, w), 0)) <= C_HALO
    head_lane = lax.broadcasted_iota(jnp.int32, (tq, LANES), 1)
    low = head_lane < C_HEAD_DIM
    cache = {}

    def window(prev_ref, cur_ref, next_ref, sub, lanes):
        lo, hi = sub * tq - C_HALO, (sub + 1) * tq + C_HALO
        parts = []
        if lo < 0:
            parts.append(prev_ref[:, lanes])
        parts.append(cur_ref[max(lo, 0):min(hi, tt), lanes])
        if hi > tt:
            parts.append(next_ref[:, lanes])
        return parts[0] if len(parts) == 1 else jnp.concatenate(parts, axis=0)

    def operands(sub, j):
        if (sub, j) not in cache:
            lanes = slice(j * LANES, (j + 1) * LANES)
            cache[(sub, j)] = (q_ref[sub * tq:(sub + 1) * tq, lanes],
                               window(kp_ref, kc_ref, kn_ref, sub, lanes),
                               window(vp_ref, vc_ref, vn_ref, sub, lanes))
        return cache[(sub, j)]

    def mask(sub):
        if ("mask", sub) not in cache:
            kpos = t0 + sub * tq - C_HALO + col
            cache[("mask", sub)] = band & (kpos >= 0) & (kpos < seq)
        return cache[("mask", sub)]

    def scores(unit):
        sub, j, e = unit
        q2 = operands(sub, j)[0]
        zero = jnp.zeros_like(q2)
        qm = jnp.where(low, q2, zero) if e == 0 else jnp.where(low, zero, q2)
        return lax.dot_general(qm, operands(sub, j)[1], NT_DIMS, preferred_element_type=F32)

    units = [(sub, j, e) for sub in range(nsub) for j in range(C_HEADS // 2) for e in range(2)]
    sc = scores(units[0])
    halves, lse_all = [], None
    for i, (sub, j, e) in enumerate(units):
        sc_next = scores(units[i + 1]) if i + 1 < len(units) else None
        sc = jnp.where(mask(sub), sc, NEG_BIG)
        m = jnp.max(sc, axis=-1, keepdims=True)
        p = jnp.exp2(sc - m)
        l = jnp.sum(p, axis=-1, keepdims=True)
        pv = jnp.dot(p.astype(BF16), operands(sub, j)[2], preferred_element_type=F32)
        halves.append(pv * (1.0 / l))
        lse = m + jnp.log2(l)
        lse_all = lse + jnp.zeros((tq, LANES), F32) if (j, e) == (0, 0) else jnp.where(
            head_lane == 2 * j + e, lse, lse_all)
        if e == 1:
            rows = slice(sub * tq, (sub + 1) * tq)
            o_ref[rows, j * LANES:(j + 1) * LANES] = jnp.where(low, halves[0], halves[1]).astype(o_ref.dtype)
            halves = []
            if j == C_HEADS // 2 - 1:
                lse_ref[rows, :] = lse_all
        sc = sc_next


def _combine_dil_kernel(o1_ref, o2_ref, o3_ref, l1_ref, l2_ref, l3_ref, o_ref, so_ref, sl_ref):
    rows = o_ref.shape[0]
    nlg = o_ref.shape[1] // LANES
    os_, ls = [], []
    for g, ((_, dil), oref, lref) in enumerate(zip(C_GROUPS, (o1_ref, o2_ref, o3_ref),
                                                   (l1_ref, l2_ref, l3_ref))):
        if dil == 1:
            os_.append(oref[0].astype(F32))
            ls.append(lref[0])
            continue
        for p in range(dil):
            og = oref[p].astype(F32)
            for j in range(nlg):
                so_ref[g - 1, j, pl.ds(p, rows // dil, stride=dil), :] = og[:, j * LANES:(j + 1) * LANES]
            sl_ref[g - 1, pl.ds(p, rows // dil, stride=dil), :] = lref[p]
        os_.append(jnp.concatenate([so_ref[g - 1, j] for j in range(nlg)], axis=-1))
        ls.append(sl_ref[g - 1])
    m = jnp.maximum(jnp.maximum(ls[0], ls[1]), ls[2])
    es = [jnp.exp2(l - m) for l in ls]
    inv = 1.0 / (es[0] + es[1] + es[2])
    width = C_HEADS * C_HEAD_DIM
    expand = (lax.broadcasted_iota(jnp.int32, (LANES, width), 1) // C_HEAD_DIM
              == lax.broadcasted_iota(jnp.int32, (LANES, width), 0)).astype(BF16)
    acc = jnp.zeros(o_ref.shape, F32)
    for e, og in zip(es, os_):
        wt = e * inv
        hi = wt.astype(BF16)
        lo = (wt - hi.astype(F32)).astype(BF16)
        wide = (jnp.dot(hi, expand, preferred_element_type=F32)
                + jnp.dot(lo, expand, preferred_element_type=F32))
        acc = acc + wide * og
    o_ref[...] = acc.astype(o_ref.dtype)


def _dilated_attention(x, gain, w_qkv, cos, sin):
    b, s, d = x.shape
    width = C_HEADS * C_HEAD_DIM
    scale = C_HEAD_DIM ** -0.5 * LOG2E
    tab = jnp.concatenate([_lane_tables(cos, sin, C_HEAD_DIM, 0, scale),
                           _lane_tables(cos, sin, C_HEAD_DIM, 0, 1.0)])
    dmax = max(dil for _, dil in C_GROUPS)
    for window, dil in C_GROUPS:
        assert (window // 2) // dil == C_HALO and (s // dil) % C_HALO == 0

    tl = _tile(s, 256)
    assert tl % (BF16_SUBLANES * dmax) == 0
    phase_spec = lambda dil, cols: pl.BlockSpec((None, dil, tl // dil, cols),
                                                lambda bi, i: (bi, 0, i, 0))
    qkv = pl.pallas_call(
        _proj_dil_kernel,
        grid=(b, s // tl),
        in_specs=[
            pl.BlockSpec((None, tl, d), lambda bi, i: (bi, i, 0)),
            _const_spec((1, d)),
            _const_spec((d, len(C_GROUPS) * 3 * width)),
            pl.BlockSpec((6, tl, LANES), lambda bi, i: (0, i, 0)),
        ],
        out_specs=[phase_spec(dil, width) for _, dil in C_GROUPS for _ in range(3)],
        out_shape=[jax.ShapeDtypeStruct((b, dil, s // dil, width), BF16)
                   for _, dil in C_GROUPS for _ in range(3)],
        scratch_shapes=[pltpu.VMEM((4, tl, LANES), F32)],
        compiler_params=_params(("parallel", "parallel")),
        name="proj_dil",
    )(x, gain.reshape(1, d), w_qkv.astype(BF16), tab)

    outs, lses = [], []
    for g, (_, dil) in enumerate(C_GROUPS):
        q, k, v = qkv[3 * g:3 * g + 3]
        seq = s // dil
        tt = _tile(seq, 512)
        tq = _tile(tt, 128)
        r = tt // C_HALO
        nhalo = seq // C_HALO
        cur = pl.BlockSpec((None, None, tt, width), lambda bi, p, i: (bi, p, i, 0))
        prev = pl.BlockSpec((None, None, C_HALO, width),
                            lambda bi, p, i, r=r: (bi, p, jnp.maximum(i * r - 1, 0), 0))
        nxt = pl.BlockSpec((None, None, C_HALO, width),
                           lambda bi, p, i, r=r, nhalo=nhalo: (bi, p, jnp.minimum((i + 1) * r, nhalo - 1), 0))
        o_g, lse_g = pl.pallas_call(
            functools.partial(_attn_dil_kernel, seq=seq, tq=tq),
            grid=(b, dil, seq // tt),
            in_specs=[cur, prev, cur, nxt, prev, cur, nxt],
            out_specs=[cur, pl.BlockSpec((None, None, tt, LANES), lambda bi, p, i: (bi, p, i, 0))],
            out_shape=[jax.ShapeDtypeStruct((b, dil, seq, width), BF16),
                       jax.ShapeDtypeStruct((b, dil, seq, LANES), F32)],
            compiler_params=_params(("parallel", "parallel", "parallel")),
            name=f"attn_dil{g}",
        )(q, k, k, k, v, v, v)
        outs.append(o_g)
        lses.append(lse_g)

    return pl.pallas_call(
        _combine_dil_kernel,
        grid=(b, s // tl),
        in_specs=([phase_spec(dil, width) for _, dil in C_GROUPS]
                  + [phase_spec(dil, LANES) for _, dil in C_GROUPS]),
        out_specs=pl.BlockSpec((None, tl, width), lambda bi, i: (bi, i, 0)),
        out_shape=jax.ShapeDtypeStruct((b, s, width), BF16),
        scratch_shapes=[pltpu.VMEM((len(C_GROUPS) - 1, width // LANES, tl, LANES), F32),
                        pltpu.VMEM((len(C_GROUPS) - 1, tl, LANES), F32)],
        compiler_params=_params(("parallel", "parallel")),
        name="combine_dil",
    )(*outs, *lses)


def _post_kernel(o_ref, x_ref, wo_ref, g_ref, wgu_ref, wout_ref, fg_ref, y_ref, acc_ref, *,
                 apply_final):
    x1 = x_ref[...] + jnp.dot(o_ref[...], wo_ref[...], preferred_element_type=F32)
    h = _rms_rows(x1, g_ref[...]).astype(BF16)
    acc_ref[...] = x1
    for c in range(FFN_DIM // FFN_CHUNK):
        lo = c * FFN_CHUNK
        gate = jnp.dot(h, wgu_ref[:, lo:lo + FFN_CHUNK], preferred_element_type=F32)
        up = jnp.dot(h, wgu_ref[:, FFN_DIM + lo:FFN_DIM + lo + FFN_CHUNK], preferred_element_type=F32)
        act = (gate * jax.nn.sigmoid(gate) * up).astype(BF16)
        acc_ref[...] += jnp.dot(act, wout_ref[lo:lo + FFN_CHUNK, :], preferred_element_type=F32)
    y = acc_ref[...]
    if apply_final:
        y = _rms_rows(y, fg_ref[...])
    y_ref[...] = y


def _post(o, x, w_o, ffn_gain, w_gu, w_out, final_gain, apply_final):
    b, s, d = x.shape
    m = b * s
    tm = _tile(m, 512)
    row = lambda i: (i, 0)
    y = pl.pallas_call(
        functools.partial(_post_kernel, apply_final=apply_final),
        grid=(m // tm,),
        in_specs=[
            pl.BlockSpec((tm, d), row),
            pl.BlockSpec((tm, d), row),
            _const_spec((d, d)),
            _const_spec((1, d)),
            _const_spec((d, 2 * FFN_DIM)),
            _const_spec((FFN_DIM, d)),
            _const_spec((1, d)),
        ],
        out_specs=pl.BlockSpec((tm, d), row),
        out_shape=jax.ShapeDtypeStruct((m, d), F32),
        scratch_shapes=[pltpu.VMEM((tm, d), F32)],
        compiler_params=_params(("parallel",)),
        name="post",
    )(o.reshape(m, d), x.reshape(m, d), w_o.astype(BF16), ffn_gain.reshape(1, d),
      w_gu.astype(BF16), w_out.astype(BF16), final_gain.reshape(1, d))
    return y.reshape(b, s, d)


def kernel(x, attn_norm, ffn_norm, final_norm, a_w_qkv, a_lambda_q1, a_lambda_k1, a_lambda_q2,
           a_lambda_k2, a_subln, a_w_o, b_w_a, b_q_norm, b_kv_norm, b_w_qb, b_w_kvb, b_w_o,
           c_w_qkv, c_w_o, f_w_gu, f_w_out):
    depth = attn_norm.shape[0]
    s = x.shape[1]
    cos_p, sin_p = _rope_tables(s, A_ROT)
    cos_b, sin_b = _rope_tables(s, B_ROPE)
    for i in range(depth):
        mixer, j = i % N_MIXERS, i // N_MIXERS
        if mixer == 0:
            lambda_init = 0.8 - 0.6 * math.exp(-0.3 * i)
            o = _diff_attention(x, attn_norm[i], a_w_qkv[j], a_lambda_q1[j], a_lambda_k1[j],
                                a_lambda_q2[j], a_lambda_k2[j], a_subln[j], lambda_init, cos_p, sin_p)
            w_o = a_w_o[j]
        elif mixer == 1:
            o = _latent_attention(x, attn_norm[i], b_w_a[j], b_q_norm[j], b_kv_norm[j], b_w_qb[j],
                                  b_w_kvb[j], cos_b, sin_b)
            w_o = b_w_o[j]
        else:
            o = _dilated_attention(x, attn_norm[i], c_w_qkv[j], cos_p, sin_p)
            w_o = c_w_o[j]
        x = _post(o, x, w_o, ffn_norm[i], f_w_gu[i], f_w_out[i], final_norm, i == depth - 1)
    return x
```

```python
import functools
import math

import jax
import jax.numpy as jnp
from jax import lax
from jax.experimental import pallas as pl
from jax.experimental.pallas import tpu as pltpu

F32 = jnp.float32
BF16 = jnp.bfloat16

D_MODEL = 1024
N_MIXERS = 3
ROPE_THETA = 500000.0
EPS = 1e-6
NEG_BIG = -1e30
LOG2E = 1.4426950408889634

A_HEADS = 8
A_HEAD_DIM = 64
A_V_DIM = 128
A_ROT = 16

B_HEADS = 16
B_Q_RANK = 384
B_KV_RANK = 256
B_NOPE = 64
B_ROPE = 32
B_V = 64

C_GROUPS = ((128, 1), (512, 4), (2048, 16))
C_HEADS = 16
C_HEAD_DIM = 64
C_ROT = 16
C_HALO = 64

FFN_DIM = 2816
FFN_CHUNK = 256

LANES = 128
BF16_SUBLANES = 16
Q_UNROLL = 2
VMEM_LIMIT_BYTES = 56 * 1024 * 1024

NT_DIMS = (((1,), (1,)), ((), ()))
BOUND_MARGIN = 1.01
MIN_DENOMINATOR = 2.0 ** -60


def _tile(n, pref):
    t = min(n, pref)
    assert n % t == 0, (n, t)
    return t


def _params(sem):
    return pltpu.CompilerParams(dimension_semantics=sem, vmem_limit_bytes=VMEM_LIMIT_BYTES)


def _const_spec(shape):
    nd = len(shape)
    return pl.BlockSpec(shape, lambda *_: (0,) * nd, pipeline_mode=pl.Buffered(1))


def _rms_rows(x, gain):
    return x * lax.rsqrt(jnp.mean(x * x, axis=-1, keepdims=True) + EPS) * gain


def _rope_tables(seq_len, rot_dim):
    pos = jnp.arange(seq_len, dtype=F32)
    inv = ROPE_THETA ** (-jnp.arange(0, rot_dim, 2, dtype=F32) / rot_dim)
    ang = pos[:, None] * inv[None, :]
    return jnp.cos(ang), jnp.sin(ang)


def _lane_tables(cos, sin, period, offset, scale):
    s, r = cos.shape
    c = jnp.ones((s, period), F32)
    c = c.at[:, offset:offset + r].set(cos).at[:, offset + r:offset + 2 * r].set(cos)
    sa = jnp.zeros((s, period), F32).at[:, offset:offset + r].set(-sin)
    sb = jnp.zeros((s, period), F32).at[:, offset + r:offset + 2 * r].set(sin)
    rep = LANES // period
    return jnp.stack([jnp.tile(t, (1, rep)) for t in (c, sa, sb)]) * scale


def _rope(x, tab_ref, base, r):
    return (x * tab_ref[base]
            + pltpu.roll(x, LANES - r, 1) * tab_ref[base + 1]
            + pltpu.roll(x, r, 1) * tab_ref[base + 2])


def _softmax_t(chains, k_ref, vt_ref, tk, shifts=None):
    rows = vt_ref.shape[-2]
    nchunk = vt_ref.shape[-3]
    state = [(jnp.full((1, qm.shape[0]), NEG_BIG, F32), jnp.zeros((1, qm.shape[0]), F32),
              jnp.zeros((rows, qm.shape[0]), F32)) for qm, _, _ in chains]
    items = [(c, n) for c in range(nchunk) for n in range(len(chains))]

    def scores(item):
        c, n = item
        qm, k_lanes, _ = chains[n]
        return lax.dot_general(k_ref[c * tk:(c + 1) * tk, k_lanes], qm, NT_DIMS,
                               preferred_element_type=F32)

    st = scores(items[0])
    for i, (c, n) in enumerate(items):
        st_next = scores(items[i + 1]) if i + 1 < len(items) else None
        m, l, acc = state[n]
        if shifts is None:
            m_new = jnp.maximum(m, jnp.max(st, axis=0, keepdims=True))
            alpha = jnp.exp2(m - m_new)
            p = jnp.exp2(st - m_new)
            l, acc = l * alpha, acc * alpha
        else:
            m_new = m
            p = jnp.exp2(st - shifts[n])
        l = l + jnp.sum(p, axis=0, keepdims=True)
        acc = acc + jnp.dot(vt_ref[chains[n][2] + (c,)], p.astype(BF16), preferred_element_type=F32)
        state[n] = (m_new, l, acc)
        st = st_next
    return [(l, acc) for _, l, acc in state]


def _key_norm_bound(k_ref, k_lanes, sel, width):
    kk = k_ref[:, k_lanes]
    kn2 = lax.dot_general(sel, kk * kk, NT_DIMS, preferred_element_type=F32)
    return jnp.broadcast_to(jnp.sqrt(jnp.max(kn2, axis=-1, keepdims=True)), (8, width))


def _score_bound(qm, kmax):
    ones = jnp.ones((8, qm.shape[1]), BF16)
    qn2 = lax.dot_general(ones, qm * qm, NT_DIMS, preferred_element_type=F32)
    return jnp.sqrt(qn2[0:1]) * kmax[0:1] * BOUND_MARGIN


def _attend_tiles(tile_fn, n_tiles, tq):
    unroll = Q_UNROLL if n_tiles % Q_UNROLL == 0 else 1

    def body(it, carry):
        row0s = [pl.multiple_of((it * unroll + u) * tq, tq) for u in range(unroll)]
        lmins = [tile_fn(r0, False) for r0 in row0s]
        for r0, lmin in zip(row0s, lmins):
            @pl.when(lmin < MIN_DENOMINATOR)
            def _():
                tile_fn(r0, True)
        return carry

    lax.fori_loop(0, n_tiles // unroll, body, 0)


def _proj_diff_kernel(x_ref, g_ref, wqk_ref, wvt_ref, tab_ref, q_ref, k_ref, vt_ref, *, tk):
    tm = x_ref.shape[0]
    h = _rms_rows(x_ref[...], g_ref[...]).astype(BF16)
    for c in range(4):
        y = jnp.dot(h, wqk_ref[:, c * 512:(c + 1) * 512], preferred_element_type=F32)
        dst, base = (q_ref, 0) if c < 2 else (k_ref, 3)
        for j in range(4):
            dst[(c % 2) * 4 + j] = _rope(y[:, j * LANES:(j + 1) * LANES], tab_ref, base,
                                         A_ROT // 2).astype(BF16)
    for half in range(2):
        vt = lax.dot_general(wvt_ref[half * 512:(half + 1) * 512, :], h, NT_DIMS,
                             preferred_element_type=F32)
        for hh in range(4):
            for c in range(tm // tk):
                vt_ref[half * 4 + hh, c] = vt[hh * A_V_DIM:(hh + 1) * A_V_DIM,
                                              c * tk:(c + 1) * tk].astype(BF16)


def _attn_diff_kernel(lam_ref, sub_ref, q_ref, k_ref, vt_ref, o_ref, *, tq, tk, lambda_init):
    low = lax.broadcasted_iota(jnp.int32, (tq, LANES), 1) < A_HEAD_DIM
    sel_low = lax.broadcasted_iota(jnp.int32, (8, LANES), 1) < A_HEAD_DIM
    kmax = [_key_norm_bound(k_ref, slice(None), sel.astype(BF16), tq)
            for sel in (sel_low, jnp.logical_not(sel_low))]
    l1 = jnp.sum(lam_ref[0:1, :] * lam_ref[1:2, :], axis=-1, keepdims=True)
    l2 = jnp.sum(lam_ref[2:3, :] * lam_ref[3:4, :], axis=-1, keepdims=True)
    lam = jnp.exp(l1) - jnp.exp(l2) + lambda_init

    def tile(row0, exact):
        q2 = q_ref[pl.ds(row0, tq), :]
        zero = jnp.zeros_like(q2)
        chains = [(jnp.where(low, q2, zero), slice(None), ()),
                  (jnp.where(low, zero, q2), slice(None), ())]
        shifts = None if exact else [_score_bound(chains[n][0], kmax[n]) for n in range(2)]
        (la, acca), (lb, accb) = _softmax_t(chains, k_ref, vt_ref, tk, shifts)
        o = (acca * (1.0 / la) - lam * (accb * (1.0 / lb))).T
        o_ref[pl.ds(row0, tq), :] = (_rms_rows(o, sub_ref[...]) * (1.0 - lambda_init)).astype(o_ref.dtype)
        return jnp.min(jnp.minimum(la, lb))

    _attend_tiles(tile, q_ref.shape[0] // tq, tq)


def _diff_attention(x, gain, w_qkv, lam_q1, lam_k1, lam_q2, lam_k2, subln, lambda_init, cos, sin):
    b, s, d = x.shape
    tq = _tile(s, 512)
    tk = _tile(s, 512)
    tm = tk
    nck = s // tk
    qk_w = 2 * A_HEADS * A_HEAD_DIM
    scale = A_HEAD_DIM ** -0.5 * LOG2E
    tab = jnp.concatenate([_lane_tables(cos, sin, A_HEAD_DIM, 0, scale),
                           _lane_tables(cos, sin, A_HEAD_DIM, 0, 1.0)])
    wqk = w_qkv[:, :2 * qk_w].astype(BF16)
    wvt = w_qkv[:, 2 * qk_w:].T.astype(BF16)

    head_major = pl.BlockSpec((None, A_HEADS, tm, LANES), lambda bi, i: (bi, 0, i, 0))
    q, k, vt = pl.pallas_call(
        functools.partial(_proj_diff_kernel, tk=tk),
        grid=(b, s // tm),
        in_specs=[
            pl.BlockSpec((None, tm, d), lambda bi, i: (bi, i, 0)),
            _const_spec((1, d)),
            _const_spec((d, 2 * qk_w)),
            _const_spec((A_HEADS * A_V_DIM, d)),
            pl.BlockSpec((6, tm, LANES), lambda bi, i: (0, i, 0)),
        ],
        out_specs=[
            head_major,
            head_major,
            pl.BlockSpec((None, A_HEADS, tm // tk, A_V_DIM, tk), lambda bi, i: (bi, 0, i, 0, 0)),
        ],
        out_shape=[
            jax.ShapeDtypeStruct((b, A_HEADS, s, LANES), BF16),
            jax.ShapeDtypeStruct((b, A_HEADS, s, LANES), BF16),
            jax.ShapeDtypeStruct((b, A_HEADS, nck, A_V_DIM, tk), BF16),
        ],
        compiler_params=_params(("parallel", "parallel")),
        name="proj_diff",
    )(x, gain.reshape(1, d), wqk, wvt, tab)

    lam = jnp.stack([lam_q1, lam_k1, lam_q2, lam_k2]).astype(F32)
    per_head = pl.BlockSpec((None, None, s, LANES), lambda bi, h: (bi, h, 0, 0))
    return pl.pallas_call(
        functools.partial(_attn_diff_kernel, tq=tq, tk=tk, lambda_init=lambda_init),
        grid=(b, A_HEADS),
        in_specs=[
            _const_spec((4, A_HEAD_DIM)),
            _const_spec((1, A_V_DIM)),
            per_head,
            per_head,
            pl.BlockSpec((None, None, nck, A_V_DIM, tk), lambda bi, h: (bi, h, 0, 0, 0)),
        ],
        out_specs=pl.BlockSpec((None, s, A_V_DIM), lambda bi, h: (bi, 0, h)),
        out_shape=jax.ShapeDtypeStruct((b, s, A_HEADS * A_V_DIM), BF16),
        compiler_params=_params(("parallel", "parallel")),
        name="attn_diff",
    )(lam, subln.reshape(1, A_V_DIM).astype(F32), q, k, vt)


def _proj_mla_kernel(x_ref, g_ref, waq_ref, wakv_ref, war_ref, qn_ref, kvn_ref, wqb_ref, wkb_ref,
                     wvt_ref, tab_ref, q_ref, k_ref, vt_ref, *, tk):
    tm = x_ref.shape[0]
    r = B_ROPE // 2
    h = _rms_rows(x_ref[...], g_ref[...]).astype(BF16)
    q_lat = _rms_rows(jnp.dot(h, waq_ref[...], preferred_element_type=F32), qn_ref[...]).astype(BF16)
    kv_lat = _rms_rows(jnp.dot(h, wakv_ref[...], preferred_element_type=F32), kvn_ref[...]).astype(BF16)
    k_rope = _rope(jnp.dot(h, war_ref[...], preferred_element_type=F32), tab_ref, 3, r)
    for c in range(4):
        yq = jnp.dot(q_lat, wqb_ref[:, c * 512:(c + 1) * 512], preferred_element_type=F32)
        yk = jnp.dot(kv_lat, wkb_ref[:, c * 512:(c + 1) * 512], preferred_element_type=F32)
        for j in range(4):
            pair, lanes = 2 * c + j // 2, slice((j % 2) * LANES, (j % 2 + 1) * LANES)
            q_ref[pair, :, lanes] = _rope(yq[:, j * LANES:(j + 1) * LANES], tab_ref, 0, r).astype(BF16)
            k_ref[pair, :, lanes] = (yk[:, j * LANES:(j + 1) * LANES] + k_rope).astype(BF16)
    for half in range(2):
        vt = lax.dot_general(wvt_ref[half * 512:(half + 1) * 512, :], kv_lat, NT_DIMS,
                             preferred_element_type=F32)
        for hp in range(4):
            for c in range(tm // tk):
                vt_ref[half * 4 + hp, c] = vt[hp * LANES:(hp + 1) * LANES, c * tk:(c + 1) * tk].astype(BF16)


def _attn_mla_kernel(q_ref, k_ref, vt_ref, o_ref, *, tq, tk):
    lanes = [slice(e * LANES, (e + 1) * LANES) for e in range(2)]
    ones = jnp.ones((8, LANES), BF16)
    kmax = [_key_norm_bound(k_ref, lanes[e], ones, tq) for e in range(2)]

    def tile(row0, exact):
        chains = [(q_ref[pl.ds(row0, tq), lanes[e]], lanes[e], ()) for e in range(2)]
        shifts = None if exact else [_score_bound(chains[n][0], kmax[n]) for n in range(2)]
        (la, acca), (lb, accb) = _softmax_t(chains, k_ref, vt_ref, tk, shifts)
        o = jnp.concatenate([acca[0:B_V] * (1.0 / la), accb[B_V:2 * B_V] * (1.0 / lb)], axis=0).T
        o_ref[pl.ds(row0, tq), :] = o.astype(o_ref.dtype)
        return jnp.min(jnp.minimum(la, lb))

    _attend_tiles(tile, q_ref.shape[0] // tq, tq)


def _latent_attention(x, gain, w_a, q_norm, kv_norm, w_qb, w_kvb, cos, sin):
    b, s, d = x.shape
    tq = _tile(s, 512)
    tk = _tile(s, 512)
    tm = tk
    nck = s // tk
    hw = B_HEADS * LANES
    npair = B_HEADS // 2
    scale = (B_NOPE + B_ROPE) ** -0.5 * LOG2E
    tab = jnp.concatenate([_lane_tables(cos, sin, LANES, B_NOPE, scale),
                           _lane_tables(cos, sin, LANES, B_NOPE, 1.0)])
    waq = w_a[:, :B_Q_RANK].astype(BF16)
    wakv = w_a[:, B_Q_RANK:B_Q_RANK + B_KV_RANK].astype(BF16)
    war = jnp.zeros((d, LANES), F32).at[:, B_NOPE:B_NOPE + B_ROPE].set(
        w_a[:, B_Q_RANK + B_KV_RANK:]).astype(BF16)
    pad = LANES - (B_NOPE + B_ROPE)
    wqb = jnp.pad(w_qb.reshape(B_Q_RANK, B_HEADS, B_NOPE + B_ROPE),
                  ((0, 0), (0, 0), (0, pad))).reshape(B_Q_RANK, hw).astype(BF16)
    wkv3 = w_kvb.reshape(B_KV_RANK, B_HEADS, B_NOPE + B_V)
    wkb = jnp.pad(wkv3[:, :, :B_NOPE], ((0, 0), (0, 0), (0, LANES - B_NOPE))
                  ).reshape(B_KV_RANK, hw).astype(BF16)
    wvt = wkv3[:, :, B_NOPE:].reshape(B_KV_RANK, B_HEADS * B_V).T.astype(BF16)

    pair_major = pl.BlockSpec((None, npair, tm, 2 * LANES), lambda bi, i: (bi, 0, i, 0))
    q, k, vt = pl.pallas_call(
        functools.partial(_proj_mla_kernel, tk=tk),
        grid=(b, s // tm),
        in_specs=[
            pl.BlockSpec((None, tm, d), lambda bi, i: (bi, i, 0)),
            _const_spec((1, d)),
            _const_spec((d, B_Q_RANK)),
            _const_spec((d, B_KV_RANK)),
            _const_spec((d, LANES)),
            _const_spec((1, B_Q_RANK)),
            _const_spec((1, B_KV_RANK)),
            _const_spec((B_Q_RANK, hw)),
            _const_spec((B_KV_RANK, hw)),
            _const_spec((B_HEADS * B_V, B_KV_RANK)),
            pl.BlockSpec((6, tm, LANES), lambda bi, i: (0, i, 0)),
        ],
        out_specs=[
            pair_major,
            pair_major,
            pl.BlockSpec((None, npair, tm // tk, 2 * B_V, tk), lambda bi, i: (bi, 0, i, 0, 0)),
        ],
        out_shape=[
            jax.ShapeDtypeStruct((b, npair, s, 2 * LANES), BF16),
            jax.ShapeDtypeStruct((b, npair, s, 2 * LANES), BF16),
            jax.ShapeDtypeStruct((b, npair, nck, 2 * B_V, tk), BF16),
        ],
        compiler_params=_params(("parallel", "parallel")),
        name="proj_mla",
    )(x, gain.reshape(1, d), waq, wakv, war, q_norm.reshape(1, -1), kv_norm.reshape(1, -1),
      wqb, wkb, wvt, tab)

    per_pair = pl.BlockSpec((None, None, s, 2 * LANES), lambda bi, h: (bi, h, 0, 0))
    return pl.pallas_call(
        functools.partial(_attn_mla_kernel, tq=tq, tk=tk),
        grid=(b, npair),
        in_specs=[
            per_pair,
            per_pair,
            pl.BlockSpec((None, None, nck, 2 * B_V, tk), lambda bi, h: (bi, h, 0, 0, 0)),
        ],
        out_specs=pl.BlockSpec((None, s, 2 * B_V), lambda bi, h: (bi, 0, h)),
        out_shape=jax.ShapeDtypeStruct((b, s, B_HEADS * B_V), BF16),
        compiler_params=_params(("parallel", "parallel")),
        name="attn_mla",
    )(q, k, vt)


def _proj_dil_kernel(x_ref, g_ref, w_ref, tab_ref, *refs):
    outs, scr = refs[:-1], refs[-1]
    tl = x_ref.shape[0]
    h = _rms_rows(x_ref[...], g_ref[...]).astype(BF16)
    width = C_HEADS * C_HEAD_DIM
    for g, (_, dil) in enumerate(C_GROUPS):
        for c in range(3 * width // 512):
            col = g * 3 * width + c * 512
            y = jnp.dot(h, w_ref[:, col:col + 512], preferred_element_type=F32)
            which = c // 2
            dst = outs[3 * g + which]
            lo = (c % 2) * 512
            for j in range(4):
                yj = y[:, j * LANES:(j + 1) * LANES]
                if which < 2:
                    yj = _rope(yj, tab_ref, 3 * which, C_ROT // 2)
                cols = slice(lo + j * LANES, lo + (j + 1) * LANES)
                if dil == 1:
                    dst[0, :, cols] = yj.astype(BF16)
                    continue
                scr[j] = yj
                for p in range(dil):
                    dst[p, :, cols] = scr[j, pl.ds(p, tl // dil, stride=dil), :].astype(BF16)


def _attn_dil_kernel(q_ref, kp_ref, kc_ref, kn_ref, vp_ref, vc_ref, vn_ref, o_ref, lse_ref, *, seq, tq):
    tt = q_ref.shape[0]
    nsub = tt // tq
    w = tq + 2 * C_HALO
    t0 = pl.program_id(2) * tt
    col = lax.broadcasted_iota(jnp.int32, (tq, w), 1)
    band = jnp.abs(col - C_HALO - lax.broadcasted_iota(jnp.int32, (tq, w), 0)) <= C_HALO
    head_lane = lax.broadcasted_iota(jnp.int32, (tq, LANES), 1)
    low = head_lane < C_HEAD_DIM
    cache = {}

    def window(prev_ref, cur_ref, next_ref, sub, lanes):
        lo, hi = sub * tq - C_HALO, (sub + 1) * tq + C_HALO
        parts = []
        if lo < 0:
            parts.append(prev_ref[:, lanes])
        parts.append(cur_ref[max(lo, 0):min(hi, tt), lanes])
        if hi > tt:
            parts.append(next_ref[:, lanes])
        return parts[0] if len(parts) == 1 else jnp.concatenate(parts, axis=0)

    def operands(sub, j):
        if (sub, j) not in cache:
            lanes = slice(j * LANES, (j + 1) * LANES)
            cache[(sub, j)] = (q_ref[sub * tq:(sub + 1) * tq, lanes],
                               window(kp_ref, kc_ref, kn_ref, sub, lanes),
                               window(vp_ref, vc_ref, vn_ref, sub, lanes))
        return cache[(sub, j)]

    def mask(sub):
        if ("mask", sub) not in cache:
            kpos = t0 + sub * tq - C_HALO + col
            cache[("mask", sub)] = band & (kpos >= 0) & (kpos < seq)
        return cache[("mask", sub)]

    def scores(unit):
        sub, j = unit
        q2 = operands(sub, j)[0]
        zero = jnp.zeros_like(q2)
        qm = jnp.concatenate([jnp.where(low, q2, zero), jnp.where(low, zero, q2)], axis=0)
        return lax.dot_general(qm, operands(sub, j)[1], NT_DIMS, preferred_element_type=F32)

    units = [(sub, j) for sub in range(nsub) for j in range(C_HEADS // 2)]
    sc = scores(units[0])
    lse_all = None
    for i, (sub, j) in enumerate(units):
        sc_next = scores(units[i + 1]) if i + 1 < len(units) else None
        msk = mask(sub)
        sc = jnp.where(jnp.concatenate([msk, msk], axis=0), sc, NEG_BIG)
        m = jnp.max(sc, axis=-1, keepdims=True)
        p = jnp.exp2(sc - m)
        l = jnp.sum(p, axis=-1, keepdims=True)
        pv = jnp.dot(p.astype(BF16), operands(sub, j)[2], preferred_element_type=F32) * (1.0 / l)
        lse = m + jnp.log2(l)
        rows = slice(sub * tq, (sub + 1) * tq)
        o_ref[rows, j * LANES:(j + 1) * LANES] = jnp.where(low, pv[0:tq], pv[tq:2 * tq]).astype(o_ref.dtype)
        for e in range(2):
            lse_e = lse[e * tq:(e + 1) * tq]
            lse_all = lse_e + jnp.zeros((tq, LANES), F32) if (j, e) == (0, 0) else jnp.where(
                head_lane == 2 * j + e, lse_e, lse_all)
        if j == C_HEADS // 2 - 1:
            lse_ref[rows, :] = lse_all
        sc = sc_next


def _combine_dil_kernel(o1_ref, o2_ref, o3_ref, l1_ref, l2_ref, l3_ref, o_ref, so_ref, sl_ref):
    rows = o_ref.shape[0]
    nlg = o_ref.shape[1] // LANES
    os_, ls = [], []
    for g, ((_, dil), oref, lref) in enumerate(zip(C_GROUPS, (o1_ref, o2_ref, o3_ref),
                                                   (l1_ref, l2_ref, l3_ref))):
        if dil == 1:
            os_.append(oref[0].astype(F32))
            ls.append(lref[0])
            continue
        for p in range(dil):
            og = oref[p].astype(F32)
            for j in range(nlg):
                so_ref[g - 1, j, pl.ds(p, rows // dil, stride=dil), :] = og[:, j * LANES:(j + 1) * LANES]
            sl_ref[g - 1, pl.ds(p, rows // dil, stride=dil), :] = lref[p]
        os_.append(jnp.concatenate([so_ref[g - 1, j] for j in range(nlg)], axis=-1))
        ls.append(sl_ref[g - 1])
    m = jnp.maximum(jnp.maximum(ls[0], ls[1]), ls[2])
    es = [jnp.exp2(l - m) for l in ls]
    inv = 1.0 / (es[0] + es[1] + es[2])
    width = C_HEADS * C_HEAD_DIM
    expand = (lax.broadcasted_iota(jnp.int32, (LANES, width), 1) // C_HEAD_DIM
              == lax.broadcasted_iota(jnp.int32, (LANES, width), 0)).astype(BF16)
    acc = jnp.zeros(o_ref.shape, F32)
    for e, og in zip(es, os_):
        wt = e * inv
        hi = wt.astype(BF16)
        lo = (wt - hi.astype(F32)).astype(BF16)
        wide = (jnp.dot(hi, expand, preferred_element_type=F32)
                + jnp.dot(lo, expand, preferred_element_type=F32))
        acc = acc + wide * og
    o_ref[...] = acc.astype(o_ref.dtype)


def _dilated_attention(x, gain, w_qkv, cos, sin):
    b, s, d = x.shape
    width = C_HEADS * C_HEAD_DIM
    scale = C_HEAD_DIM ** -0.5 * LOG2E
    tab = jnp.concatenate([_lane_tables(cos, sin, C_HEAD_DIM, 0, scale),
                           _lane_tables(cos, sin, C_HEAD_DIM, 0, 1.0)])
    dmax = max(dil for _, dil in C_GROUPS)
    for window, dil in C_GROUPS:
        assert (window // 2) // dil == C_HALO and (s // dil) % C_HALO == 0

    tl = _tile(s, 512)
    assert tl % (BF16_SUBLANES * dmax) == 0
    phase_spec = lambda dil, cols: pl.BlockSpec((None, dil, tl // dil, cols),
                                                lambda bi, i: (bi, 0, i, 0))
    qkv = pl.pallas_call(
        _proj_dil_kernel,
        grid=(b, s // tl),
        in_specs=[
            pl.BlockSpec((None, tl, d), lambda bi, i: (bi, i, 0)),
            _const_spec((1, d)),
            _const_spec((d, len(C_GROUPS) * 3 * width)),
            pl.BlockSpec((6, tl, LANES), lambda bi, i: (0, i, 0)),
        ],
        out_specs=[phase_spec(dil, width) for _, dil in C_GROUPS for _ in range(3)],
        out_shape=[jax.ShapeDtypeStruct((b, dil, s // dil, width), BF16)
                   for _, dil in C_GROUPS for _ in range(3)],
        scratch_shapes=[pltpu.VMEM((4, tl, LANES), F32)],
        compiler_params=_params(("parallel", "parallel")),
        name="proj_dil",
    )(x, gain.reshape(1, d), w_qkv.astype(BF16), tab)

    outs, lses = [], []
    for g, (_, dil) in enumerate(C_GROUPS):
        q, k, v = qkv[3 * g:3 * g + 3]
        seq = s // dil
        tt = _tile(seq, 512)
        tq = _tile(tt, 128)
        r = tt // C_HALO
        nhalo = seq // C_HALO
        cur = pl.BlockSpec((None, None, tt, width), lambda bi, p, i: (bi, p, i, 0))
        prev = pl.BlockSpec((None, None, C_HALO, width),
                            lambda bi, p, i, r=r: (bi, p, jnp.maximum(i * r - 1, 0), 0))
        nxt = pl.BlockSpec((None, None, C_HALO, width),
                           lambda bi, p, i, r=r, nhalo=nhalo: (bi, p, jnp.minimum((i + 1) * r, nhalo - 1), 0))
        o_g, lse_g = pl.pallas_call(
            functools.partial(_attn_dil_kernel, seq=seq, tq=tq),
            grid=(b, dil, seq // tt),
            in_specs=[cur, prev, cur, nxt, prev, cur, nxt],
            out_specs=[cur, pl.BlockSpec((None, None, tt, LANES), lambda bi, p, i: (bi, p, i, 0))],
            out_shape=[jax.ShapeDtypeStruct((b, dil, seq, width), BF16),
                       jax.ShapeDtypeStruct((b, dil, seq, LANES), F32)],
            compiler_params=_params(("parallel", "parallel", "parallel")),
            name=f"attn_dil{g}",
        )(q, k, k, k, v, v, v)
        outs.append(o_g)
        lses.append(lse_g)

    return pl.pallas_call(
        _combine_dil_kernel,
        grid=(b, s // tl),
        in_specs=([phase_spec(dil, width) for _, dil in C_GROUPS]
                  + [phase_spec(dil, LANES) for _, dil in C_GROUPS]),
        out_specs=pl.BlockSpec((None, tl, width), lambda bi, i: (bi, i, 0)),
        out_shape=jax.ShapeDtypeStruct((b, s, width), BF16),
        scratch_shapes=[pltpu.VMEM((len(C_GROUPS) - 1, width // LANES, tl, LANES), F32),
                        pltpu.VMEM((len(C_GROUPS) - 1, tl, LANES), F32)],
        compiler_params=_params(("parallel", "parallel")),
        name="combine_dil",
    )(*outs, *lses)


def _post_kernel(o_ref, x_ref, wo_ref, g_ref, wgu_ref, wout_ref, fg_ref, y_ref, acc_ref, *,
                 apply_final):
    x1 = x_ref[...] + jnp.dot(o_ref[...], wo_ref[...], preferred_element_type=F32)
    h = _rms_rows(x1, g_ref[...]).astype(BF16)
    acc_ref[...] = x1
    for c in range(FFN_DIM // FFN_CHUNK):
        lo = c * FFN_CHUNK
        gate = jnp.dot(h, wgu_ref[:, lo:lo + FFN_CHUNK], preferred_element_type=F32)
        up = jnp.dot(h, wgu_ref[:, FFN_DIM + lo:FFN_DIM + lo + FFN_CHUNK], preferred_element_type=F32)
        act = (gate * jax.nn.sigmoid(gate) * up).astype(BF16)
        acc_ref[...] += jnp.dot(act, wout_ref[lo:lo + FFN_CHUNK, :], preferred_element_type=F32)
    y = acc_ref[...]
    if apply_final:
        y = _rms_rows(y, fg_ref[...])
    y_ref[...] = y


def _post(o, x, w_o, ffn_gain, w_gu, w_out, final_gain, apply_final):
    b, s, d = x.shape
    m = b * s
    tm = _tile(m, 512)
    row = lambda i: (i, 0)
    y = pl.pallas_call(
        functools.partial(_post_kernel, apply_final=apply_final),
        grid=(m // tm,),
        in_specs=[
            pl.BlockSpec((tm, d), row),
            pl.BlockSpec((tm, d), row),
            _const_spec((d, d)),
            _const_spec((1, d)),
            _const_spec((d, 2 * FFN_DIM)),
            _const_spec((FFN_DIM, d)),
            _const_spec((1, d)),
        ],
        out_specs=pl.BlockSpec((tm, d), row),
        out_shape=jax.ShapeDtypeStruct((m, d), F32),
        scratch_shapes=[pltpu.VMEM((tm, d), F32)],
        compiler_params=_params(("parallel",)),
        name="post",
    )(o.reshape(m, d), x.reshape(m, d), w_o.astype(BF16), ffn_gain.reshape(1, d),
      w_gu.astype(BF16), w_out.astype(BF16), final_gain.reshape(1, d))
    return y.reshape(b, s, d)


def kernel(x, attn_norm, ffn_norm, final_norm, a_w_qkv, a_lambda_q1, a_lambda_k1, a_lambda_q2,
           a_lambda_k2, a_subln, a_w_o, b_w_a, b_q_norm, b_kv_norm, b_w_qb, b_w_kvb, b_w_o,
           c_w_qkv, c_w_o, f_w_gu, f_w_out):
    depth = attn_norm.shape[0]
    s = x.shape[1]
    cos_p, sin_p = _rope_tables(s, A_ROT)
    cos_b, sin_b = _rope_tables(s, B_ROPE)
    for i in range(depth):
        mixer, j = i % N_MIXERS, i // N_MIXERS
        if mixer == 0:
            lambda_init = 0.8 - 0.6 * math.exp(-0.3 * i)
            o = _diff_attention(x, attn_norm[i], a_w_qkv[j], a_lambda_q1[j], a_lambda_k1[j],
                                a_lambda_q2[j], a_lambda_k2[j], a_subln[j], lambda_init, cos_p, sin_p)
            w_o = a_w_o[j]
        elif mixer == 1:
            o = _latent_attention(x, attn_norm[i], b_w_a[j], b_q_norm[j], b_kv_norm[j], b_w_qb[j],
                                  b_w_kvb[j], cos_b, sin_b)
            w_o = b_w_o[j]
        else:
            o = _dilated_attention(x, attn_norm[i], c_w_qkv[j], cos_p, sin_p)
            w_o = c_w_o[j]
        x = _post(o, x, w_o, ffn_norm[i], f_w_gu[i], f_w_out[i], final_norm, i == depth - 1)
    return x
```

```python
import functools
import math

import jax
import jax.numpy as jnp
from jax import lax
from jax.experimental import pallas as pl
from jax.experimental.pallas import tpu as pltpu

F32 = jnp.float32
BF16 = jnp.bfloat16

D_MODEL = 1024
N_MIXERS = 3
ROPE_THETA = 500000.0
EPS = 1e-6
NEG_BIG = -1e30
LOG2E = 1.4426950408889634

A_HEADS = 8
A_HEAD_DIM = 64
A_V_DIM = 128
A_ROT = 16

B_HEADS = 16
B_Q_RANK = 384
B_KV_RANK = 256
B_NOPE = 64
B_ROPE = 32
B_V = 64

C_GROUPS = ((128, 1), (512, 4), (2048, 16))
C_HEADS = 16
C_HEAD_DIM = 64
C_ROT = 16
C_HALO = 64

FFN_DIM = 2816
FFN_CHUNK = 256

LANES = 128
BF16_SUBLANES = 16
Q_UNROLL = 2
VMEM_LIMIT_BYTES = 56 * 1024 * 1024

NT_DIMS = (((1,), (1,)), ((), ()))
BOUND_MARGIN = 1.01
MIN_DENOMINATOR = 2.0 ** -60


def _tile(n, pref):
    t = min(n, pref)
    assert n % t == 0, (n, t)
    return t


def _params(sem):
    return pltpu.CompilerParams(dimension_semantics=sem, vmem_limit_bytes=VMEM_LIMIT_BYTES)


def _const_spec(shape):
    nd = len(shape)
    return pl.BlockSpec(shape, lambda *_: (0,) * nd, pipeline_mode=pl.Buffered(1))


def _rms_rows(x, gain):
    return x * lax.rsqrt(jnp.mean(x * x, axis=-1, keepdims=True) + EPS) * gain


def _rope_tables(seq_len, rot_dim):
    pos = jnp.arange(seq_len, dtype=F32)
    inv = ROPE_THETA ** (-jnp.arange(0, rot_dim, 2, dtype=F32) / rot_dim)
    ang = pos[:, None] * inv[None, :]
    return jnp.cos(ang), jnp.sin(ang)


def _lane_tables(cos, sin, period, offset, scale):
    s, r = cos.shape
    c = jnp.ones((s, period), F32)
    c = c.at[:, offset:offset + r].set(cos).at[:, offset + r:offset + 2 * r].set(cos)
    sa = jnp.zeros((s, period), F32).at[:, offset:offset + r].set(-sin)
    sb = jnp.zeros((s, period), F32).at[:, offset + r:offset + 2 * r].set(sin)
    rep = LANES // period
    return jnp.stack([jnp.tile(t, (1, rep)) for t in (c, sa, sb)]) * scale


def _rope(x, tab_ref, base, r):
    return (x * tab_ref[base]
            + pltpu.roll(x, LANES - r, 1) * tab_ref[base + 1]
            + pltpu.roll(x, r, 1) * tab_ref[base + 2])


def _softmax_t(chains, k_ref, vt_ref, tk, shifts=None):
    rows = vt_ref.shape[-2]
    nchunk = vt_ref.shape[-3]
    state = [(jnp.full((1, qm.shape[0]), NEG_BIG, F32), jnp.zeros((1, qm.shape[0]), F32),
              jnp.zeros((rows, qm.shape[0]), F32)) for qm, _, _ in chains]
    items = [(c, n) for c in range(nchunk) for n in range(len(chains))]

    def scores(item):
        c, n = item
        qm, k_lanes, _ = chains[n]
        return lax.dot_general(k_ref[c * tk:(c + 1) * tk, k_lanes], qm, NT_DIMS,
                               preferred_element_type=F32)

    st = scores(items[0])
    for i, (c, n) in enumerate(items):
        st_next = scores(items[i + 1]) if i + 1 < len(items) else None
        m, l, acc = state[n]
        if shifts is None:
            m_new = jnp.maximum(m, jnp.max(st, axis=0, keepdims=True))
            alpha = jnp.exp2(m - m_new)
            p = jnp.exp2(st - m_new)
            l, acc = l * alpha, acc * alpha
        else:
            m_new = m
            p = jnp.exp2(st - shifts[n])
        l = l + jnp.sum(p, axis=0, keepdims=True)
        acc = acc + jnp.dot(vt_ref[chains[n][2] + (c,)], p.astype(BF16), preferred_element_type=F32)
        state[n] = (m_new, l, acc)
        st = st_next
    return [(l, acc) for _, l, acc in state]


def _key_norm_bound(k_ref, k_lanes, sel, width):
    kk = k_ref[:, k_lanes]
    kn2 = lax.dot_general(sel, kk * kk, NT_DIMS, preferred_element_type=F32)
    return jnp.broadcast_to(jnp.sqrt(jnp.max(kn2, axis=-1, keepdims=True)), (8, width))


def _score_bound(qm, kmax):
    ones = jnp.ones((8, qm.shape[1]), BF16)
    qn2 = lax.dot_general(ones, qm * qm, NT_DIMS, preferred_element_type=F32)
    return jnp.sqrt(qn2[0:1]) * kmax[0:1] * BOUND_MARGIN


def _attend_tiles(tile_fn, n_tiles, tq):
    unroll = Q_UNROLL if n_tiles % Q_UNROLL == 0 else 1

    def body(it, carry):
        row0s = [pl.multiple_of((it * unroll + u) * tq, tq) for u in range(unroll)]
        lmins = [tile_fn(r0, False) for r0 in row0s]
        for r0, lmin in zip(row0s, lmins):
            @pl.when(lmin < MIN_DENOMINATOR)
            def _():
                tile_fn(r0, True)
        return carry

    lax.fori_loop(0, n_tiles // unroll, body, 0)


def _vt_chunk_spec(heads, rows, tm, tk):
    if tm >= tk:
        return pl.BlockSpec((None, heads, tm // tk, rows, tk), lambda bi, i: (bi, 0, i, 0, 0))
    per = tk // tm
    return pl.BlockSpec((None, heads, 1, rows, tm), lambda bi, i: (bi, 0, i // per, 0, i % per))


def _proj_diff_kernel(x_ref, g_ref, wqk_ref, wvt_ref, tab_ref, q_ref, k_ref, vt_ref):
    cw = vt_ref.shape[-1]
    h = _rms_rows(x_ref[...], g_ref[...]).astype(BF16)
    for c in range(4):
        y = jnp.dot(h, wqk_ref[:, c * 512:(c + 1) * 512], preferred_element_type=F32)
        dst, base = (q_ref, 0) if c < 2 else (k_ref, 3)
        for j in range(4):
            dst[(c % 2) * 4 + j] = _rope(y[:, j * LANES:(j + 1) * LANES], tab_ref, base,
                                         A_ROT // 2).astype(BF16)
    for half in range(2):
        vt = lax.dot_general(wvt_ref[half * 512:(half + 1) * 512, :], h, NT_DIMS,
                             preferred_element_type=F32)
        for hh in range(4):
            for c in range(vt_ref.shape[1]):
                vt_ref[half * 4 + hh, c] = vt[hh * A_V_DIM:(hh + 1) * A_V_DIM,
                                              c * cw:(c + 1) * cw].astype(BF16)


def _attn_diff_kernel(lam_ref, sub_ref, q_ref, k_ref, vt_ref, o_ref, *, tq, tk, lambda_init):
    low = lax.broadcasted_iota(jnp.int32, (tq, LANES), 1) < A_HEAD_DIM
    sel_low = lax.broadcasted_iota(jnp.int32, (8, LANES), 1) < A_HEAD_DIM
    kmax = [_key_norm_bound(k_ref, slice(None), sel.astype(BF16), tq)
            for sel in (sel_low, jnp.logical_not(sel_low))]
    l1 = jnp.sum(lam_ref[0:1, :] * lam_ref[1:2, :], axis=-1, keepdims=True)
    l2 = jnp.sum(lam_ref[2:3, :] * lam_ref[3:4, :], axis=-1, keepdims=True)
    lam = jnp.exp(l1) - jnp.exp(l2) + lambda_init

    def tile(row0, exact):
        q2 = q_ref[pl.ds(row0, tq), :]
        zero = jnp.zeros_like(q2)
        chains = [(jnp.where(low, q2, zero), slice(None), ()),
                  (jnp.where(low, zero, q2), slice(None), ())]
        shifts = None if exact else [_score_bound(chains[n][0], kmax[n]) for n in range(2)]
        (la, acca), (lb, accb) = _softmax_t(chains, k_ref, vt_ref, tk, shifts)
        o = (acca * (1.0 / la) - lam * (accb * (1.0 / lb))).T
        o_ref[pl.ds(row0, tq), :] = (_rms_rows(o, sub_ref[...]) * (1.0 - lambda_init)).astype(o_ref.dtype)
        return jnp.min(jnp.minimum(la, lb))

    _attend_tiles(tile, q_ref.shape[0] // tq, tq)


def _diff_attention(x, gain, w_qkv, lam_q1, lam_k1, lam_q2, lam_k2, subln, lambda_init, cos, sin):
    b, s, d = x.shape
    tq = _tile(s, 512)
    tk = _tile(s, 4096)
    tm = _tile(s, 1024)
    nck = s // tk
    qk_w = 2 * A_HEADS * A_HEAD_DIM
    scale = A_HEAD_DIM ** -0.5 * LOG2E
    tab = jnp.concatenate([_lane_tables(cos, sin, A_HEAD_DIM, 0, scale),
                           _lane_tables(cos, sin, A_HEAD_DIM, 0, 1.0)])
    wqk = w_qkv[:, :2 * qk_w].astype(BF16)
    wvt = w_qkv[:, 2 * qk_w:].T.astype(BF16)

    head_major = pl.BlockSpec((None, A_HEADS, tm, LANES), lambda bi, i: (bi, 0, i, 0))
    q, k, vt = pl.pallas_call(
        _proj_diff_kernel,
        grid=(b, s // tm),
        in_specs=[
            pl.BlockSpec((None, tm, d), lambda bi, i: (bi, i, 0)),
            _const_spec((1, d)),
            _const_spec((d, 2 * qk_w)),
            _const_spec((A_HEADS * A_V_DIM, d)),
            pl.BlockSpec((6, tm, LANES), lambda bi, i: (0, i, 0)),
        ],
        out_specs=[
            head_major,
            head_major,
            _vt_chunk_spec(A_HEADS, A_V_DIM, tm, tk),
        ],
        out_shape=[
            jax.ShapeDtypeStruct((b, A_HEADS, s, LANES), BF16),
            jax.ShapeDtypeStruct((b, A_HEADS, s, LANES), BF16),
            jax.ShapeDtypeStruct((b, A_HEADS, nck, A_V_DIM, tk), BF16),
        ],
        compiler_params=_params(("parallel", "parallel")),
        name="proj_diff",
    )(x, gain.reshape(1, d), wqk, wvt, tab)

    lam = jnp.stack([lam_q1, lam_k1, lam_q2, lam_k2]).astype(F32)
    per_head = pl.BlockSpec((None, None, s, LANES), lambda bi, h: (bi, h, 0, 0))
    return pl.pallas_call(
        functools.partial(_attn_diff_kernel, tq=tq, tk=tk, lambda_init=lambda_init),
        grid=(b, A_HEADS),
        in_specs=[
            _const_spec((4, A_HEAD_DIM)),
            _const_spec((1, A_V_DIM)),
            per_head,
            per_head,
            pl.BlockSpec((None, None, nck, A_V_DIM, tk), lambda bi, h: (bi, h, 0, 0, 0)),
        ],
        out_specs=pl.BlockSpec((None, s, A_V_DIM), lambda bi, h: (bi, 0, h)),
        out_shape=jax.ShapeDtypeStruct((b, s, A_HEADS * A_V_DIM), BF16),
        compiler_params=_params(("parallel", "parallel")),
        name="attn_diff",
    )(lam, subln.reshape(1, A_V_DIM).astype(F32), q, k, vt)


def _proj_mla_kernel(x_ref, g_ref, waq_ref, wakv_ref, war_ref, qn_ref, kvn_ref, wqb_ref, wkb_ref,
                     wvt_ref, tab_ref, q_ref, k_ref, vt_ref):
    cw = vt_ref.shape[-1]
    r = B_ROPE // 2
    h = _rms_rows(x_ref[...], g_ref[...]).astype(BF16)
    q_lat = _rms_rows(jnp.dot(h, waq_ref[...], preferred_element_type=F32), qn_ref[...]).astype(BF16)
    kv_lat = _rms_rows(jnp.dot(h, wakv_ref[...], preferred_element_type=F32), kvn_ref[...]).astype(BF16)
    k_rope = _rope(jnp.dot(h, war_ref[...], preferred_element_type=F32), tab_ref, 3, r)
    for c in range(4):
        yq = jnp.dot(q_lat, wqb_ref[:, c * 512:(c + 1) * 512], preferred_element_type=F32)
        yk = jnp.dot(kv_lat, wkb_ref[:, c * 512:(c + 1) * 512], preferred_element_type=F32)
        for j in range(4):
            pair, lanes = 2 * c + j // 2, slice((j % 2) * LANES, (j % 2 + 1) * LANES)
            q_ref[pair, :, lanes] = _rope(yq[:, j * LANES:(j + 1) * LANES], tab_ref, 0, r).astype(BF16)
            k_ref[pair, :, lanes] = (yk[:, j * LANES:(j + 1) * LANES] + k_rope).astype(BF16)
    for half in range(2):
        vt = lax.dot_general(wvt_ref[half * 512:(half + 1) * 512, :], kv_lat, NT_DIMS,
                             preferred_element_type=F32)
        for hp in range(4):
            for c in range(vt_ref.shape[1]):
                vt_ref[half * 4 + hp, c] = vt[hp * LANES:(hp + 1) * LANES, c * cw:(c + 1) * cw].astype(BF16)


def _attn_mla_kernel(q_ref, k_ref, vt_ref, o_ref, *, tq, tk):
    lanes = [slice(e * LANES, (e + 1) * LANES) for e in range(2)]
    ones = jnp.ones((8, LANES), BF16)
    kmax = [_key_norm_bound(k_ref, lanes[e], ones, tq) for e in range(2)]

    def tile(row0, exact):
        chains = [(q_ref[pl.ds(row0, tq), lanes[e]], lanes[e], ()) for e in range(2)]
        shifts = None if exact else [_score_bound(chains[n][0], kmax[n]) for n in range(2)]
        (la, acca), (lb, accb) = _softmax_t(chains, k_ref, vt_ref, tk, shifts)
        o = jnp.concatenate([acca[0:B_V] * (1.0 / la), accb[B_V:2 * B_V] * (1.0 / lb)], axis=0).T
        o_ref[pl.ds(row0, tq), :] = o.astype(o_ref.dtype)
        return jnp.min(jnp.minimum(la, lb))

    _attend_tiles(tile, q_ref.shape[0] // tq, tq)


def _latent_attention(x, gain, w_a, q_norm, kv_norm, w_qb, w_kvb, cos, sin):
    b, s, d = x.shape
    tq = _tile(s, 512)
    tk = _tile(s, 4096)
    tm = _tile(s, 1024)
    nck = s // tk
    hw = B_HEADS * LANES
    npair = B_HEADS // 2
    scale = (B_NOPE + B_ROPE) ** -0.5 * LOG2E
    tab = jnp.concatenate([_lane_tables(cos, sin, LANES, B_NOPE, scale),
                           _lane_tables(cos, sin, LANES, B_NOPE, 1.0)])
    waq = w_a[:, :B_Q_RANK].astype(BF16)
    wakv = w_a[:, B_Q_RANK:B_Q_RANK + B_KV_RANK].astype(BF16)
    war = jnp.zeros((d, LANES), F32).at[:, B_NOPE:B_NOPE + B_ROPE].set(
        w_a[:, B_Q_RANK + B_KV_RANK:]).astype(BF16)
    pad = LANES - (B_NOPE + B_ROPE)
    wqb = jnp.pad(w_qb.reshape(B_Q_RANK, B_HEADS, B_NOPE + B_ROPE),
                  ((0, 0), (0, 0), (0, pad))).reshape(B_Q_RANK, hw).astype(BF16)
    wkv3 = w_kvb.reshape(B_KV_RANK, B_HEADS, B_NOPE + B_V)
    wkb = jnp.pad(wkv3[:, :, :B_NOPE], ((0, 0), (0, 0), (0, LANES - B_NOPE))
                  ).reshape(B_KV_RANK, hw).astype(BF16)
    wvt = wkv3[:, :, B_NOPE:].reshape(B_KV_RANK, B_HEADS * B_V).T.astype(BF16)

    pair_major = pl.BlockSpec((None, npair, tm, 2 * LANES), lambda bi, i: (bi, 0, i, 0))
    q, k, vt = pl.pallas_call(
        _proj_mla_kernel,
        grid=(b, s // tm),
        in_specs=[
            pl.BlockSpec((None, tm, d), lambda bi, i: (bi, i, 0)),
            _const_spec((1, d)),
            _const_spec((d, B_Q_RANK)),
            _const_spec((d, B_KV_RANK)),
            _const_spec((d, LANES)),
            _const_spec((1, B_Q_RANK)),
            _const_spec((1, B_KV_RANK)),
            _const_spec((B_Q_RANK, hw)),
            _const_spec((B_KV_RANK, hw)),
            _const_spec((B_HEADS * B_V, B_KV_RANK)),
            pl.BlockSpec((6, tm, LANES), lambda bi, i: (0, i, 0)),
        ],
        out_specs=[
            pair_major,
            pair_major,
            _vt_chunk_spec(npair, 2 * B_V, tm, tk),
        ],
        out_shape=[
            jax.ShapeDtypeStruct((b, npair, s, 2 * LANES), BF16),
            jax.ShapeDtypeStruct((b, npair, s, 2 * LANES), BF16),
            jax.ShapeDtypeStruct((b, npair, nck, 2 * B_V, tk), BF16),
        ],
        compiler_params=_params(("parallel", "parallel")),
        name="proj_mla",
    )(x, gain.reshape(1, d), waq, wakv, war, q_norm.reshape(1, -1), kv_norm.reshape(1, -1),
      wqb, wkb, wvt, tab)

    per_pair = pl.BlockSpec((None, None, s, 2 * LANES), lambda bi, h: (bi, h, 0, 0))
    return pl.pallas_call(
        functools.partial(_attn_mla_kernel, tq=tq, tk=tk),
        grid=(b, npair),
        in_specs=[
            per_pair,
            per_pair,
            pl.BlockSpec((None, None, nck, 2 * B_V, tk), lambda bi, h: (bi, h, 0, 0, 0)),
        ],
        out_specs=pl.BlockSpec((None, s, 2 * B_V), lambda bi, h: (bi, 0, h)),
        out_shape=jax.ShapeDtypeStruct((b, s, B_HEADS * B_V), BF16),
        compiler_params=_params(("parallel", "parallel")),
        name="attn_mla",
    )(q, k, vt)


def _proj_dil_kernel(x_ref, g_ref, w_ref, tab_ref, *refs):
    outs, scr = refs[:-1], refs[-1]
    tl = x_ref.shape[0]
    h = _rms_rows(x_ref[...], g_ref[...]).astype(BF16)
    width = C_HEADS * C_HEAD_DIM
    for g, (_, dil) in enumerate(C_GROUPS):
        for c in range(3 * width // 512):
            col = g * 3 * width + c * 512
            y = jnp.dot(h, w_ref[:, col:col + 512], preferred_element_type=F32)
            which = c // 2
            dst = outs[3 * g + which]
            lo = (c % 2) * 512
            for j in range(4):
                yj = y[:, j * LANES:(j + 1) * LANES]
                if which < 2:
                    yj = _rope(yj, tab_ref, 3 * which, C_ROT // 2)
                cols = slice(lo + j * LANES, lo + (j + 1) * LANES)
                if dil == 1:
                    dst[0, :, cols] = yj.astype(BF16)
                    continue
                scr[j] = yj
                for p in range(dil):
                    dst[p, :, cols] = scr[j, pl.ds(p, tl // dil, stride=dil), :].astype(BF16)


def _attn_dil_kernel(q_ref, kp_ref, kc_ref, kn_ref, vp_ref, vc_ref, vn_ref, o_ref, lse_ref, *, seq, tq):
    tt = q_ref.shape[0]
    nsub = tt // tq
    w = tq + 2 * C_HALO
    t0 = pl.program_id(2) * tt
    col = lax.broadcasted_iota(jnp.int32, (tq, w), 1)
    band = jnp.abs(col - C_HALO - lax.broadcasted_iota(jnp.int32, (tq, w), 0)) <= C_HALO
    head_lane = lax.broadcasted_iota(jnp.int32, (tq, LANES), 1)
    low = head_lane < C_HEAD_DIM
    cache = {}

    def window(prev_ref, cur_ref, next_ref, sub, lanes):
        lo, hi = sub * tq - C_HALO, (sub + 1) * tq + C_HALO
        parts = []
        if lo < 0:
            parts.append(prev_ref[:, lanes])
        parts.append(cur_ref[max(lo, 0):min(hi, tt), lanes])
        if hi > tt:
            parts.append(next_ref[:, lanes])
        return parts[0] if len(parts) == 1 else jnp.concatenate(parts, axis=0)

    def operands(sub, j):
        if (sub, j) not in cache:
            lanes = slice(j * LANES, (j + 1) * LANES)
            cache[(sub, j)] = (q_ref[sub * tq:(sub + 1) * tq, lanes],
                               window(kp_ref, kc_ref, kn_ref, sub, lanes),
                               window(vp_ref, vc_ref, vn_ref, sub, lanes))
        return cache[(sub, j)]

    def mask(sub):
        if ("mask", sub) not in cache:
            kpos = t0 + sub * tq - C_HALO + col
            cache[("mask", sub)] = band & (kpos >= 0) & (kpos < seq)
        return cache[("mask", sub)]

    def scores(unit):
        sub, j = unit
        q2 = operands(sub, j)[0]
        zero = jnp.zeros_like(q2)
        qm = jnp.concatenate([jnp.where(low, q2, zero), jnp.where(low, zero, q2)], axis=0)
        return lax.dot_general(qm, operands(sub, j)[1], NT_DIMS, preferred_element_type=F32)

    units = [(sub, j) for sub in range(nsub) for j in range(C_HEADS // 2)]
    sc = scores(units[0])
    lse_all = None
    for i, (sub, j) in enumerate(units):
        sc_next = scores(units[i + 1]) if i + 1 < len(units) else None
        msk = mask(sub)
        sc = jnp.where(jnp.concatenate([msk, msk], axis=0), sc, NEG_BIG)
        m = jnp.max(sc, axis=-1, keepdims=True)
        p = jnp.exp2(sc - m)
        l = jnp.sum(p, axis=-1, keepdims=True)
        pv = jnp.dot(p.astype(BF16), operands(sub, j)[2], preferred_element_type=F32) * (1.0 / l)
        lse = m + jnp.log2(l)
        rows = slice(sub * tq, (sub + 1) * tq)
        o_ref[rows, j * LANES:(j + 1) * LANES] = jnp.where(low, pv[0:tq], pv[tq:2 * tq]).astype(o_ref.dtype)
        for e in range(2):
            lse_e = lse[e * tq:(e + 1) * tq]
            lse_all = lse_e + jnp.zeros((tq, LANES), F32) if (j, e) == (0, 0) else jnp.where(
                head_lane == 2 * j + e, lse_e, lse_all)
        if j == C_HEADS // 2 - 1:
            lse_ref[rows, :] = lse_all
        sc = sc_next


def _combine_dil_kernel(o1_ref, o2_ref, o3_ref, l1_ref, l2_ref, l3_ref, o_ref, so_ref, sl_ref):
    rows = o_ref.shape[0]
    nlg = o_ref.shape[1] // LANES
    os_, ls = [], []
    for g, ((_, dil), oref, lref) in enumerate(zip(C_GROUPS, (o1_ref, o2_ref, o3_ref),
                                                   (l1_ref, l2_ref, l3_ref))):
        if dil == 1:
            os_.append(oref[0].astype(F32))
            ls.append(lref[0])
            continue
        for p in range(dil):
            og = oref[p].astype(F32)
            for j in range(nlg):
                so_ref[g - 1, j, pl.ds(p, rows // dil, stride=dil), :] = og[:, j * LANES:(j + 1) * LANES]
            sl_ref[g - 1, pl.ds(p, rows // dil, stride=dil), :] = lref[p]
        os_.append(jnp.concatenate([so_ref[g - 1, j] for j in range(nlg)], axis=-1))
        ls.append(sl_ref[g - 1])
    m = jnp.maximum(jnp.maximum(ls[0], ls[1]), ls[2])
    es = [jnp.exp2(l - m) for l in ls]
    inv = 1.0 / (es[0] + es[1] + es[2])
    width = C_HEADS * C_HEAD_DIM
    expand = (lax.broadcasted_iota(jnp.int32, (LANES, width), 1) // C_HEAD_DIM
              == lax.broadcasted_iota(jnp.int32, (LANES, width), 0)).astype(BF16)
    acc = jnp.zeros(o_ref.shape, F32)
    for e, og in zip(es, os_):
        wt = e * inv
        hi = wt.astype(BF16)
        lo = (wt - hi.astype(F32)).astype(BF16)
        wide = (jnp.dot(hi, expand, preferred_element_type=F32)
                + jnp.dot(lo, expand, preferred_element_type=F32))
        acc = acc + wide * og
    o_ref[...] = acc.astype(o_ref.dtype)


def _dilated_attention(x, gain, w_qkv, cos, sin):
    b, s, d = x.shape
    width = C_HEADS * C_HEAD_DIM
    scale = C_HEAD_DIM ** -0.5 * LOG2E
    tab = jnp.concatenate([_lane_tables(cos, sin, C_HEAD_DIM, 0, scale),
                           _lane_tables(cos, sin, C_HEAD_DIM, 0, 1.0)])
    dmax = max(dil for _, dil in C_GROUPS)
    for window, dil in C_GROUPS:
        assert (window // 2) // dil == C_HALO and (s // dil) % C_HALO == 0

    tl = _tile(s, 512)
    assert tl % (BF16_SUBLANES * dmax) == 0
    phase_spec = lambda dil, cols: pl.BlockSpec((None, dil, tl // dil, cols),
                                                lambda bi, i: (bi, 0, i, 0))
    qkv = pl.pallas_call(
        _proj_dil_kernel,
        grid=(b, s // tl),
        in_specs=[
            pl.BlockSpec((None, tl, d), lambda bi, i: (bi, i, 0)),
            _const_spec((1, d)),
            _const_spec((d, len(C_GROUPS) * 3 * width)),
            pl.BlockSpec((6, tl, LANES), lambda bi, i: (0, i, 0)),
        ],
        out_specs=[phase_spec(dil, width) for _, dil in C_GROUPS for _ in range(3)],
        out_shape=[jax.ShapeDtypeStruct((b, dil, s // dil, width), BF16)
                   for _, dil in C_GROUPS for _ in range(3)],
        scratch_shapes=[pltpu.VMEM((4, tl, LANES), F32)],
        compiler_params=_params(("parallel", "parallel")),
        name="proj_dil",
    )(x, gain.reshape(1, d), w_qkv.astype(BF16), tab)

    outs, lses = [], []
    for g, (_, dil) in enumerate(C_GROUPS):
        q, k, v = qkv[3 * g:3 * g + 3]
        seq = s // dil
        tt = _tile(seq, 512)
        tq = _tile(tt, 128)
        r = tt // C_HALO
        nhalo = seq // C_HALO
        cur = pl.BlockSpec((None, None, tt, width), lambda bi, p, i: (bi, p, i, 0))
        prev = pl.BlockSpec((None, None, C_HALO, width),
                            lambda bi, p, i, r=r: (bi, p, jnp.maximum(i * r - 1, 0), 0))
        nxt = pl.BlockSpec((None, None, C_HALO, width),
                           lambda bi, p, i, r=r, nhalo=nhalo: (bi, p, jnp.minimum((i + 1) * r, nhalo - 1), 0))
        o_g, lse_g = pl.pallas_call(
            functools.partial(_attn_dil_kernel, seq=seq, tq=tq),
            grid=(b, dil, seq // tt),
            in_specs=[cur, prev, cur, nxt, prev, cur, nxt],
            out_specs=[cur, pl.BlockSpec((None, None, tt, LANES), lambda bi, p, i: (bi, p, i, 0))],
            out_shape=[jax.ShapeDtypeStruct((b, dil, seq, width), BF16),
                       jax.ShapeDtypeStruct((b, dil, seq, LANES), F32)],
            compiler_params=_params(("parallel", "parallel", "parallel")),
            name=f"attn_dil{g}",
        )(q, k, k, k, v, v, v)
        outs.append(o_g)
        lses.append(lse_g)

    return pl.pallas_call(
        _combine_dil_kernel,
        grid=(b, s // tl),
        in_specs=([phase_spec(dil, width) for _, dil in C_GROUPS]
                  + [phase_spec(dil, LANES) for _, dil in C_GROUPS]),
        out_specs=pl.BlockSpec((None, tl, width), lambda bi, i: (bi, i, 0)),
        out_shape=jax.ShapeDtypeStruct((b, s, width), BF16),
        scratch_shapes=[pltpu.VMEM((len(C_GROUPS) - 1, width // LANES, tl, LANES), F32),
                        pltpu.VMEM((len(C_GROUPS) - 1, tl, LANES), F32)],
        compiler_params=_params(("parallel", "parallel")),
        name="combine_dil",
    )(*outs, *lses)


def _post_kernel(o_ref, x_ref, wo_ref, g_ref, wgu_ref, wout_ref, fg_ref, y_ref, acc_ref, *,
                 apply_final):
    x1 = x_ref[...] + jnp.dot(o_ref[...], wo_ref[...], preferred_element_type=F32)
    h = _rms_rows(x1, g_ref[...]).astype(BF16)
    acc_ref[...] = x1
    for c in range(FFN_DIM // FFN_CHUNK):
        lo = c * FFN_CHUNK
        gate = jnp.dot(h, wgu_ref[:, lo:lo + FFN_CHUNK], preferred_element_type=F32)
        up = jnp.dot(h, wgu_ref[:, FFN_DIM + lo:FFN_DIM + lo + FFN_CHUNK], preferred_element_type=F32)
        act = (gate * jax.nn.sigmoid(gate) * up).astype(BF16)
        acc_ref[...] += jnp.dot(act, wout_ref[lo:lo + FFN_CHUNK, :], preferred_element_type=F32)
    y = acc_ref[...]
    if apply_final:
        y = _rms_rows(y, fg_ref[...])
    y_ref[...] = y


def _post(o, x, w_o, ffn_gain, w_gu, w_out, final_gain, apply_final):
    b, s, d = x.shape
    m = b * s
    tm = _tile(m, 512)
    row = lambda i: (i, 0)
    y = pl.pallas_call(
        functools.partial(_post_kernel, apply_final=apply_final),
        grid=(m // tm,),
        in_specs=[
            pl.BlockSpec((tm, d), row),
            pl.BlockSpec((tm, d), row),
            _const_spec((d, d)),
            _const_spec((1, d)),
            _const_spec((d, 2 * FFN_DIM)),
            _const_spec((FFN_DIM, d)),
            _const_spec((1, d)),
        ],
        out_specs=pl.BlockSpec((tm, d), row),
        out_shape=jax.ShapeDtypeStruct((m, d), F32),
        scratch_shapes=[pltpu.VMEM((tm, d), F32)],
        compiler_params=_params(("parallel",)),
        name="post",
    )(o.reshape(m, d), x.reshape(m, d), w_o.astype(BF16), ffn_gain.reshape(1, d),
      w_gu.astype(BF16), w_out.astype(BF16), final_gain.reshape(1, d))
    return y.reshape(b, s, d)


def kernel(x, attn_norm, ffn_norm, final_norm, a_w_qkv, a_lambda_q1, a_lambda_k1, a_lambda_q2,
           a_lambda_k2, a_subln, a_w_o, b_w_a, b_q_norm, b_kv_norm, b_w_qb, b_w_kvb, b_w_o,
           c_w_qkv, c_w_o, f_w_gu, f_w_out):
    depth = attn_norm.shape[0]
    s = x.shape[1]
    cos_p, sin_p = _rope_tables(s, A_ROT)
    cos_b, sin_b = _rope_tables(s, B_ROPE)
    for i in range(depth):
        mixer, j = i % N_MIXERS, i // N_MIXERS
        if mixer == 0:
            lambda_init = 0.8 - 0.6 * math.exp(-0.3 * i)
            o = _diff_attention(x, attn_norm[i], a_w_qkv[j], a_lambda_q1[j], a_lambda_k1[j],
                                a_lambda_q2[j], a_lambda_k2[j], a_subln[j], lambda_init, cos_p, sin_p)
            w_o = a_w_o[j]
        elif mixer == 1:
            o = _latent_attention(x, attn_norm[i], b_w_a[j], b_q_norm[j], b_kv_norm[j], b_w_qb[j],
                                  b_w_kvb[j], cos_b, sin_b)
            w_o = b_w_o[j]
        else:
            o = _dilated_attention(x, attn_norm[i], c_w_qkv[j], cos_p, sin_p)
            w_o = c_w_o[j]
        x = _post(o, x, w_o, ffn_norm[i], f_w_gu[i], f_w_out[i], final_norm, i == depth - 1)
    return x
```

```python
import functools
import math

import jax
import jax.numpy as jnp
from jax import lax
from jax.experimental import pallas as pl
from jax.experimental.pallas import tpu as pltpu

F32 = jnp.float32
BF16 = jnp.bfloat16

D_MODEL = 1024
N_MIXERS = 3
ROPE_THETA = 500000.0
EPS = 1e-6
NEG_BIG = -1e30
LOG2E = 1.4426950408889634

A_HEADS = 8
A_HEAD_DIM = 64
A_V_DIM = 128
A_ROT = 16

B_HEADS = 16
B_Q_RANK = 384
B_KV_RANK = 256
B_NOPE = 64
B_ROPE = 32
B_V = 64

C_GROUPS = ((128, 1), (512, 4), (2048, 16))
C_HEADS = 16
C_HEAD_DIM = 64
C_ROT = 16
C_HALO = 64

FFN_DIM = 2816
FFN_CHUNK = 256

LANES = 128
BF16_SUBLANES = 16
Q_UNROLL = 2
VMEM_LIMIT_BYTES = 56 * 1024 * 1024

NT_DIMS = (((1,), (1,)), ((), ()))
BOUND_MARGIN = 1.01
MIN_DENOMINATOR = 2.0 ** -60


def _tile(n, pref):
    t = min(n, pref)
    assert n % t == 0, (n, t)
    return t


def _params(sem):
    return pltpu.CompilerParams(dimension_semantics=sem, vmem_limit_bytes=VMEM_LIMIT_BYTES)


def _const_spec(shape):
    nd = len(shape)
    return pl.BlockSpec(shape, lambda *_: (0,) * nd, pipeline_mode=pl.Buffered(1))


def _rms_rows(x, gain):
    return x * lax.rsqrt(jnp.mean(x * x, axis=-1, keepdims=True) + EPS) * gain


def _rope_tables(seq_len, rot_dim):
    pos = jnp.arange(seq_len, dtype=F32)
    inv = ROPE_THETA ** (-jnp.arange(0, rot_dim, 2, dtype=F32) / rot_dim)
    ang = pos[:, None] * inv[None, :]
    return jnp.cos(ang), jnp.sin(ang)


def _lane_tables(cos, sin, period, offset, scale):
    s, r = cos.shape
    c = jnp.ones((s, period), F32)
    c = c.at[:, offset:offset + r].set(cos).at[:, offset + r:offset + 2 * r].set(cos)
    sa = jnp.zeros((s, period), F32).at[:, offset:offset + r].set(-sin)
    sb = jnp.zeros((s, period), F32).at[:, offset + r:offset + 2 * r].set(sin)
    rep = LANES // period
    return jnp.stack([jnp.tile(t, (1, rep)) for t in (c, sa, sb)]) * scale


def _rope(x, tab_ref, base, r):
    return (x * tab_ref[base]
            + pltpu.roll(x, LANES - r, 1) * tab_ref[base + 1]
            + pltpu.roll(x, r, 1) * tab_ref[base + 2])


def _softmax_t(chains, k_ref, vt_ref, tk, shifts=None):
    rows = vt_ref.shape[-2]
    nchunk = vt_ref.shape[-3]
    state = [(jnp.full((1, qm.shape[0]), NEG_BIG, F32), jnp.zeros((1, qm.shape[0]), F32),
              jnp.zeros((rows, qm.shape[0]), F32)) for qm, _, _ in chains]
    items = [(c, n) for c in range(nchunk) for n in range(len(chains))]

    def scores(item):
        c, n = item
        qm, k_lanes, _ = chains[n]
        return lax.dot_general(k_ref[c * tk:(c + 1) * tk, k_lanes], qm, NT_DIMS,
                               preferred_element_type=F32)

    st = scores(items[0])
    for i, (c, n) in enumerate(items):
        st_next = scores(items[i + 1]) if i + 1 < len(items) else None
        m, l, acc = state[n]
        if shifts is None:
            m_new = jnp.maximum(m, jnp.max(st, axis=0, keepdims=True))
            alpha = jnp.exp2(m - m_new)
            p = jnp.exp2(st - m_new)
            l, acc = l * alpha, acc * alpha
        else:
            m_new = m
            p = jnp.exp2(st - shifts[n])
        l = l + jnp.sum(p, axis=0, keepdims=True)
        acc = acc + jnp.dot(vt_ref[chains[n][2] + (c,)], p.astype(BF16), preferred_element_type=F32)
        state[n] = (m_new, l, acc)
        st = st_next
    return [(l, acc) for _, l, acc in state]


def _key_norm_bound(k_ref, k_lanes, sel, width):
    kk = k_ref[:, k_lanes]
    kn2 = lax.dot_general(sel, kk * kk, NT_DIMS, preferred_element_type=F32)
    return jnp.broadcast_to(jnp.sqrt(jnp.max(kn2, axis=-1, keepdims=True)), (8, width))


def _score_bound(qm, kmax):
    ones = jnp.ones((8, qm.shape[1]), BF16)
    qn2 = lax.dot_general(ones, qm * qm, NT_DIMS, preferred_element_type=F32)
    return jnp.sqrt(qn2[0:1]) * kmax[0:1] * BOUND_MARGIN


def _attend_tiles(tile_fn, n_tiles, tq):
    unroll = Q_UNROLL if n_tiles % Q_UNROLL == 0 else 1

    def body(it, carry):
        row0s = [pl.multiple_of((it * unroll + u) * tq, tq) for u in range(unroll)]
        lmins = [tile_fn(r0, False) for r0 in row0s]
        for r0, lmin in zip(row0s, lmins):
            @pl.when(lmin < MIN_DENOMINATOR)
            def _():
                tile_fn(r0, True)
        return carry

    lax.fori_loop(0, n_tiles // unroll, body, 0)


def _vt_chunk_spec(heads, rows, tm, tk):
    if tm >= tk:
        return pl.BlockSpec((None, heads, tm // tk, rows, tk), lambda bi, i: (bi, 0, i, 0, 0))
    per = tk // tm
    return pl.BlockSpec((None, heads, 1, rows, tm), lambda bi, i: (bi, 0, i // per, 0, i % per))


def _proj_diff_kernel(x_ref, g_ref, wqk_ref, wvt_ref, tab_ref, q_ref, k_ref, vt_ref):
    cw = vt_ref.shape[-1]
    h = _rms_rows(x_ref[...], g_ref[...]).astype(BF16)
    for c in range(4):
        y = jnp.dot(h, wqk_ref[:, c * 512:(c + 1) * 512], preferred_element_type=F32)
        dst, base = (q_ref, 0) if c < 2 else (k_ref, 3)
        for j in range(4):
            dst[(c % 2) * 4 + j] = _rope(y[:, j * LANES:(j + 1) * LANES], tab_ref, base,
                                         A_ROT // 2).astype(BF16)
    for half in range(2):
        vt = lax.dot_general(wvt_ref[half * 512:(half + 1) * 512, :], h, NT_DIMS,
                             preferred_element_type=F32)
        for hh in range(4):
            for c in range(vt_ref.shape[1]):
                vt_ref[half * 4 + hh, c] = vt[hh * A_V_DIM:(hh + 1) * A_V_DIM,
                                              c * cw:(c + 1) * cw].astype(BF16)


def _attn_diff_kernel(lam_ref, sub_ref, q_ref, k_ref, vt_ref, o_ref, p_ref, *, tq, tk, lambda_init):
    nck = vt_ref.shape[0]
    low = lax.broadcasted_iota(jnp.int32, (tq, LANES), 1) < A_HEAD_DIM
    sel_low = lax.broadcasted_iota(jnp.int32, (8, LANES), 1) < A_HEAD_DIM
    kmax = [_key_norm_bound(k_ref, slice(None), sel.astype(BF16), tq)
            for sel in (sel_low, jnp.logical_not(sel_low))]
    l1 = jnp.sum(lam_ref[0:1, :] * lam_ref[1:2, :], axis=-1, keepdims=True)
    l2 = jnp.sum(lam_ref[2:3, :] * lam_ref[3:4, :], axis=-1, keepdims=True)
    lam = jnp.exp(l1) - jnp.exp(l2) + lambda_init
    items = [(c, n) for c in range(nck) for n in range(2)]

    def tile(row0, exact):
        q2 = q_ref[pl.ds(row0, tq), :]
        zero = jnp.zeros_like(q2)
        qms = (jnp.where(low, q2, zero), jnp.where(low, zero, q2))

        def scores(item):
            c, n = item
            return lax.dot_general(k_ref[c * tk:(c + 1) * tk, :], qms[n], NT_DIMS,
                                   preferred_element_type=F32)

        if exact:
            shifts = [jnp.full((1, tq), NEG_BIG, F32), jnp.full((1, tq), NEG_BIG, F32)]
            for c, n in items:
                shifts[n] = jnp.maximum(shifts[n], jnp.max(scores((c, n)), axis=0, keepdims=True))
        else:
            shifts = [_score_bound(qms[n], kmax[n]) for n in range(2)]
        dens = [jnp.zeros((1, tq), F32), jnp.zeros((1, tq), F32)]
        st = scores(items[0])
        for i, (c, n) in enumerate(items):
            st_next = scores(items[i + 1]) if i + 1 < len(items) else None
            p = jnp.exp2(st - shifts[n])
            p_ref[n, c] = p.astype(BF16)
            dens[n] = dens[n] + jnp.sum(p, axis=0, keepdims=True)
            st = st_next
        la, lb = dens
        ratio = (lam * la / lb).astype(BF16)
        acc = jnp.zeros((A_V_DIM, tq), F32)
        for c in range(nck):
            pt = p_ref[0, c] - p_ref[1, c] * ratio
            acc = acc + jnp.dot(vt_ref[c], pt, preferred_element_type=F32)
        o = (acc * (1.0 / la)).T
        o_ref[pl.ds(row0, tq), :] = (_rms_rows(o, sub_ref[...]) * (1.0 - lambda_init)).astype(o_ref.dtype)
        return jnp.min(jnp.minimum(la, lb))

    _attend_tiles(tile, q_ref.shape[0] // tq, tq)


def _diff_attention(x, gain, w_qkv, lam_q1, lam_k1, lam_q2, lam_k2, subln, lambda_init, cos, sin):
    b, s, d = x.shape
    tq = _tile(s, 512)
    tk = _tile(s, 512)
    tm = _tile(s, 1024)
    nck = s // tk
    qk_w = 2 * A_HEADS * A_HEAD_DIM
    scale = A_HEAD_DIM ** -0.5 * LOG2E
    tab = jnp.concatenate([_lane_tables(cos, sin, A_HEAD_DIM, 0, scale),
                           _lane_tables(cos, sin, A_HEAD_DIM, 0, 1.0)])
    wqk = w_qkv[:, :2 * qk_w].astype(BF16)
    wvt = w_qkv[:, 2 * qk_w:].T.astype(BF16)

    head_major = pl.BlockSpec((None, A_HEADS, tm, LANES), lambda bi, i: (bi, 0, i, 0))
    q, k, vt = pl.pallas_call(
        _proj_diff_kernel,
        grid=(b, s // tm),
        in_specs=[
            pl.BlockSpec((None, tm, d), lambda bi, i: (bi, i, 0)),
            _const_spec((1, d)),
            _const_spec((d, 2 * qk_w)),
            _const_spec((A_HEADS * A_V_DIM, d)),
            pl.BlockSpec((6, tm, LANES), lambda bi, i: (0, i, 0)),
        ],
        out_specs=[
            head_major,
            head_major,
            _vt_chunk_spec(A_HEADS, A_V_DIM, tm, tk),
        ],
        out_shape=[
            jax.ShapeDtypeStruct((b, A_HEADS, s, LANES), BF16),
            jax.ShapeDtypeStruct((b, A_HEADS, s, LANES), BF16),
            jax.ShapeDtypeStruct((b, A_HEADS, nck, A_V_DIM, tk), BF16),
        ],
        compiler_params=_params(("parallel", "parallel")),
        name="proj_diff",
    )(x, gain.reshape(1, d), wqk, wvt, tab)

    lam = jnp.stack([lam_q1, lam_k1, lam_q2, lam_k2]).astype(F32)
    per_head = pl.BlockSpec((None, None, s, LANES), lambda bi, h: (bi, h, 0, 0))
    return pl.pallas_call(
        functools.partial(_attn_diff_kernel, tq=tq, tk=tk, lambda_init=lambda_init),
        grid=(b, A_HEADS),
        in_specs=[
            _const_spec((4, A_HEAD_DIM)),
            _const_spec((1, A_V_DIM)),
            per_head,
            per_head,
            pl.BlockSpec((None, None, nck, A_V_DIM, tk), lambda bi, h: (bi, h, 0, 0, 0)),
        ],
        out_specs=pl.BlockSpec((None, s, A_V_DIM), lambda bi, h: (bi, 0, h)),
        out_shape=jax.ShapeDtypeStruct((b, s, A_HEADS * A_V_DIM), BF16),
        scratch_shapes=[pltpu.VMEM((2, nck, tk, tq), BF16)],
        compiler_params=_params(("parallel", "parallel")),
        name="attn_diff",
    )(lam, subln.reshape(1, A_V_DIM).astype(F32), q, k, vt)


def _proj_mla_kernel(x_ref, g_ref, waq_ref, wakv_ref, war_ref, qn_ref, kvn_ref, wqb_ref, wkb_ref,
                     wvt_ref, tab_ref, q_ref, k_ref, vt_ref):
    cw = vt_ref.shape[-1]
    r = B_ROPE // 2
    h = _rms_rows(x_ref[...], g_ref[...]).astype(BF16)
    q_lat = _rms_rows(jnp.dot(h, waq_ref[...], preferred_element_type=F32), qn_ref[...]).astype(BF16)
    kv_lat = _rms_rows(jnp.dot(h, wakv_ref[...], preferred_element_type=F32), kvn_ref[...]).astype(BF16)
    k_rope = _rope(jnp.dot(h, war_ref[...], preferred_element_type=F32), tab_ref, 3, r)
    for c in range(4):
        yq = jnp.dot(q_lat, wqb_ref[:, c * 512:(c + 1) * 512], preferred_element_type=F32)
        yk = jnp.dot(kv_lat, wkb_ref[:, c * 512:(c + 1) * 512], preferred_element_type=F32)
        for j in range(4):
            pair, lanes = 2 * c + j // 2, slice((j % 2) * LANES, (j % 2 + 1) * LANES)
            q_ref[pair, :, lanes] = _rope(yq[:, j * LANES:(j + 1) * LANES], tab_ref, 0, r).astype(BF16)
            k_ref[pair, :, lanes] = (yk[:, j * LANES:(j + 1) * LANES] + k_rope).astype(BF16)
    for half in range(2):
        vt = lax.dot_general(wvt_ref[half * 512:(half + 1) * 512, :], kv_lat, NT_DIMS,
                             preferred_element_type=F32)
        for hp in range(4):
            for c in range(vt_ref.shape[1]):
                vt_ref[half * 4 + hp, c] = vt[hp * LANES:(hp + 1) * LANES, c * cw:(c + 1) * cw].astype(BF16)


def _attn_mla_kernel(q_ref, k_ref, vt_ref, o_ref, *, tq, tk):
    lanes = [slice(e * LANES, (e + 1) * LANES) for e in range(2)]
    ones = jnp.ones((8, LANES), BF16)
    kmax = [_key_norm_bound(k_ref, lanes[e], ones, tq) for e in range(2)]

    def tile(row0, exact):
        chains = [(q_ref[pl.ds(row0, tq), lanes[e]], lanes[e], ()) for e in range(2)]
        shifts = None if exact else [_score_bound(chains[n][0], kmax[n]) for n in range(2)]
        (la, acca), (lb, accb) = _softmax_t(chains, k_ref, vt_ref, tk, shifts)
        o = jnp.concatenate([acca[0:B_V] * (1.0 / la), accb[B_V:2 * B_V] * (1.0 / lb)], axis=0).T
        o_ref[pl.ds(row0, tq), :] = o.astype(o_ref.dtype)
        return jnp.min(jnp.minimum(la, lb))

    _attend_tiles(tile, q_ref.shape[0] // tq, tq)


def _latent_attention(x, gain, w_a, q_norm, kv_norm, w_qb, w_kvb, cos, sin):
    b, s, d = x.shape
    tq = _tile(s, 512)
    tk = _tile(s, 4096)
    tm = _tile(s, 1024)
    nck = s // tk
    hw = B_HEADS * LANES
    npair = B_HEADS // 2
    scale = (B_NOPE + B_ROPE) ** -0.5 * LOG2E
    tab = jnp.concatenate([_lane_tables(cos, sin, LANES, B_NOPE, scale),
                           _lane_tables(cos, sin, LANES, B_NOPE, 1.0)])
    waq = w_a[:, :B_Q_RANK].astype(BF16)
    wakv = w_a[:, B_Q_RANK:B_Q_RANK + B_KV_RANK].astype(BF16)
    war = jnp.zeros((d, LANES), F32).at[:, B_NOPE:B_NOPE + B_ROPE].set(
        w_a[:, B_Q_RANK + B_KV_RANK:]).astype(BF16)
    pad = LANES - (B_NOPE + B_ROPE)
    wqb = jnp.pad(w_qb.reshape(B_Q_RANK, B_HEADS, B_NOPE + B_ROPE),
                  ((0, 0), (0, 0), (0, pad))).reshape(B_Q_RANK, hw).astype(BF16)
    wkv3 = w_kvb.reshape(B_KV_RANK, B_HEADS, B_NOPE + B_V)
    wkb = jnp.pad(wkv3[:, :, :B_NOPE], ((0, 0), (0, 0), (0, LANES - B_NOPE))
                  ).reshape(B_KV_RANK, hw).astype(BF16)
    wvt = wkv3[:, :, B_NOPE:].reshape(B_KV_RANK, B_HEADS * B_V).T.astype(BF16)

    pair_major = pl.BlockSpec((None, npair, tm, 2 * LANES), lambda bi, i: (bi, 0, i, 0))
    q, k, vt = pl.pallas_call(
        _proj_mla_kernel,
        grid=(b, s // tm),
        in_specs=[
            pl.BlockSpec((None, tm, d), lambda bi, i: (bi, i, 0)),
            _const_spec((1, d)),
            _const_spec((d, B_Q_RANK)),
            _const_spec((d, B_KV_RANK)),
            _const_spec((d, LANES)),
            _const_spec((1, B_Q_RANK)),
            _const_spec((1, B_KV_RANK)),
            _const_spec((B_Q_RANK, hw)),
            _const_spec((B_KV_RANK, hw)),
            _const_spec((B_HEADS * B_V, B_KV_RANK)),
            pl.BlockSpec((6, tm, LANES), lambda bi, i: (0, i, 0)),
        ],
        out_specs=[
            pair_major,
            pair_major,
            _vt_chunk_spec(npair, 2 * B_V, tm, tk),
        ],
        out_shape=[
            jax.ShapeDtypeStruct((b, npair, s, 2 * LANES), BF16),
            jax.ShapeDtypeStruct((b, npair, s, 2 * LANES), BF16),
            jax.ShapeDtypeStruct((b, npair, nck, 2 * B_V, tk), BF16),
        ],
        compiler_params=_params(("parallel", "parallel")),
        name="proj_mla",
    )(x, gain.reshape(1, d), waq, wakv, war, q_norm.reshape(1, -1), kv_norm.reshape(1, -1),
      wqb, wkb, wvt, tab)

    per_pair = pl.BlockSpec((None, None, s, 2 * LANES), lambda bi, h: (bi, h, 0, 0))
    return pl.pallas_call(
        functools.partial(_attn_mla_kernel, tq=tq, tk=tk),
        grid=(b, npair),
        in_specs=[
            per_pair,
            per_pair,
            pl.BlockSpec((None, None, nck, 2 * B_V, tk), lambda bi, h: (bi, h, 0, 0, 0)),
        ],
        out_specs=pl.BlockSpec((None, s, 2 * B_V), lambda bi, h: (bi, 0, h)),
        out_shape=jax.ShapeDtypeStruct((b, s, B_HEADS * B_V), BF16),
        compiler_params=_params(("parallel", "parallel")),
        name="attn_mla",
    )(q, k, vt)


def _proj_dil_kernel(x_ref, g_ref, w_ref, tab_ref, *refs):
    outs, scr = refs[:-1], refs[-1]
    tl = x_ref.shape[0]
    h = _rms_rows(x_ref[...], g_ref[...]).astype(BF16)
    width = C_HEADS * C_HEAD_DIM
    for g, (_, dil) in enumerate(C_GROUPS):
        for c in range(3 * width // 512):
            col = g * 3 * width + c * 512
            y = jnp.dot(h, w_ref[:, col:col + 512], preferred_element_type=F32)
            which = c // 2
            dst = outs[3 * g + which]
            lo = (c % 2) * 512
            for j in range(4):
                yj = y[:, j * LANES:(j + 1) * LANES]
                if which < 2:
                    yj = _rope(yj, tab_ref, 3 * which, C_ROT // 2)
                cols = slice(lo + j * LANES, lo + (j + 1) * LANES)
                if dil == 1:
                    dst[0, :, cols] = yj.astype(BF16)
                    continue
                scr[j] = yj
                for p in range(dil):
                    dst[p, :, cols] = scr[j, pl.ds(p, tl // dil, stride=dil), :].astype(BF16)


def _attn_dil_kernel(q_ref, kp_ref, kc_ref, kn_ref, vp_ref, vc_ref, vn_ref, o_ref, lse_ref, *, seq, tq):
    tt = q_ref.shape[0]
    nsub = tt // tq
    w = tq + 2 * C_HALO
    t0 = pl.program_id(2) * tt
    col = lax.broadcasted_iota(jnp.int32, (tq, w), 1)
    band = jnp.abs(col - C_HALO - lax.broadcasted_iota(jnp.int32, (tq, w), 0)) <= C_HALO
    head_lane = lax.broadcasted_iota(jnp.int32, (tq, LANES), 1)
    low = head_lane < C_HEAD_DIM
    cache = {}

    def window(prev_ref, cur_ref, next_ref, sub, lanes):
        lo, hi = sub * tq - C_HALO, (sub + 1) * tq + C_HALO
        parts = []
        if lo < 0:
            parts.append(prev_ref[:, lanes])
        parts.append(cur_ref[max(lo, 0):min(hi, tt), lanes])
        if hi > tt:
            parts.append(next_ref[:, lanes])
        return parts[0] if len(parts) == 1 else jnp.concatenate(parts, axis=0)

    def operands(sub, j):
        if (sub, j) not in cache:
            lanes = slice(j * LANES, (j + 1) * LANES)
            cache[(sub, j)] = (q_ref[sub * tq:(sub + 1) * tq, lanes],
                               window(kp_ref, kc_ref, kn_ref, sub, lanes),
                               window(vp_ref, vc_ref, vn_ref, sub, lanes))
        return cache[(sub, j)]

    def mask(sub):
        if ("mask", sub) not in cache:
            kpos = t0 + sub * tq - C_HALO + col
            cache[("mask", sub)] = band & (kpos >= 0) & (kpos < seq)
        return cache[("mask", sub)]

    def scores(unit):
        sub, j = unit
        q2 = operands(sub, j)[0]
        zero = jnp.zeros_like(q2)
        qm = jnp.concatenate([jnp.where(low, q2, zero), jnp.where(low, zero, q2)], axis=0)
        return lax.dot_general(qm, operands(sub, j)[1], NT_DIMS, preferred_element_type=F32)

    units = [(sub, j) for sub in range(nsub) for j in range(C_HEADS // 2)]
    sc = scores(units[0])
    lse_all = None
    for i, (sub, j) in enumerate(units):
        sc_next = scores(units[i + 1]) if i + 1 < len(units) else None
        msk = mask(sub)
        sc = jnp.where(jnp.concatenate([msk, msk], axis=0), sc, NEG_BIG)
        m = jnp.max(sc, axis=-1, keepdims=True)
        p = jnp.exp2(sc - m)
        l = jnp.sum(p, axis=-1, keepdims=True)
        pv = jnp.dot(p.astype(BF16), operands(sub, j)[2], preferred_element_type=F32) * (1.0 / l)
        lse = m + jnp.log2(l)
        rows = slice(sub * tq, (sub + 1) * tq)
        o_ref[rows, j * LANES:(j + 1) * LANES] = jnp.where(low, pv[0:tq], pv[tq:2 * tq]).astype(o_ref.dtype)
        for e in range(2):
            lse_e = lse[e * tq:(e + 1) * tq]
            lse_all = lse_e + jnp.zeros((tq, LANES), F32) if (j, e) == (0, 0) else jnp.where(
                head_lane == 2 * j + e, lse_e, lse_all)
        if j == C_HEADS // 2 - 1:
            lse_ref[rows, :] = lse_all
        sc = sc_next


def _combine_dil_kernel(o1_ref, o2_ref, o3_ref, l1_ref, l2_ref, l3_ref, o_ref, so_ref, sl_ref):
    rows = o_ref.shape[0]
    nlg = o_ref.shape[1] // LANES
    os_, ls = [], []
    for g, ((_, dil), oref, lref) in enumerate(zip(C_GROUPS, (o1_ref, o2_ref, o3_ref),
                                                   (l1_ref, l2_ref, l3_ref))):
        if dil == 1:
            os_.append(oref[0].astype(F32))
            ls.append(lref[0])
            continue
        for p in range(dil):
            og = oref[p].astype(F32)
            for j in range(nlg):
                so_ref[g - 1, j, pl.ds(p, rows // dil, stride=dil), :] = og[:, j * LANES:(j + 1) * LANES]
            sl_ref[g - 1, pl.ds(p, rows // dil, stride=dil), :] = lref[p]
        os_.append(jnp.concatenate([so_ref[g - 1, j] for j in range(nlg)], axis=-1))
        ls.append(sl_ref[g - 1])
    m = jnp.maximum(jnp.maximum(ls[0], ls[1]), ls[2])
    es = [jnp.exp2(l - m) for l in ls]
    inv = 1.0 / (es[0] + es[1] + es[2])
    width = C_HEADS * C_HEAD_DIM
    expand = (lax.broadcasted_iota(jnp.int32, (LANES, width), 1) // C_HEAD_DIM
              == lax.broadcasted_iota(jnp.int32, (LANES, width), 0)).astype(BF16)
    acc = jnp.zeros(o_ref.shape, F32)
    for e, og in zip(es, os_):
        wt = e * inv
        hi = wt.astype(BF16)
        lo = (wt - hi.astype(F32)).astype(BF16)
        wide = (jnp.dot(hi, expand, preferred_element_type=F32)
                + jnp.dot(lo, expand, preferred_element_type=F32))
        acc = acc + wide * og
    o_ref[...] = acc.astype(o_ref.dtype)


def _dilated_attention(x, gain, w_qkv, cos, sin):
    b, s, d = x.shape
    width = C_HEADS * C_HEAD_DIM
    scale = C_HEAD_DIM ** -0.5 * LOG2E
    tab = jnp.concatenate([_lane_tables(cos, sin, C_HEAD_DIM, 0, scale),
                           _lane_tables(cos, sin, C_HEAD_DIM, 0, 1.0)])
    dmax = max(dil for _, dil in C_GROUPS)
    for window, dil in C_GROUPS:
        assert (window // 2) // dil == C_HALO and (s // dil) % C_HALO == 0

    tl = _tile(s, 512)
    assert tl % (BF16_SUBLANES * dmax) == 0
    phase_spec = lambda dil, cols: pl.BlockSpec((None, dil, tl // dil, cols),
                                                lambda bi, i: (bi, 0, i, 0))
    qkv = pl.pallas_call(
        _proj_dil_kernel,
        grid=(b, s // tl),
        in_specs=[
            pl.BlockSpec((None, tl, d), lambda bi, i: (bi, i, 0)),
            _const_spec((1, d)),
            _const_spec((d, len(C_GROUPS) * 3 * width)),
            pl.BlockSpec((6, tl, LANES), lambda bi, i: (0, i, 0)),
        ],
        out_specs=[phase_spec(dil, width) for _, dil in C_GROUPS for _ in range(3)],
        out_shape=[jax.ShapeDtypeStruct((b, dil, s // dil, width), BF16)
                   for _, dil in C_GROUPS for _ in range(3)],
        scratch_shapes=[pltpu.VMEM((4, tl, LANES), F32)],
        compiler_params=_params(("parallel", "parallel")),
        name="proj_dil",
    )(x, gain.reshape(1, d), w_qkv.astype(BF16), tab)

    outs, lses = [], []
    for g, (_, dil) in enumerate(C_GROUPS):
        q, k, v = qkv[3 * g:3 * g + 3]
        seq = s // dil
        tt = _tile(seq, 512)
        tq = _tile(tt, 128)
        r = tt // C_HALO
        nhalo = seq // C_HALO
        cur = pl.BlockSpec((None, None, tt, width), lambda bi, p, i: (bi, p, i, 0))
        prev = pl.BlockSpec((None, None, C_HALO, width),
                            lambda bi, p, i, r=r: (bi, p, jnp.maximum(i * r - 1, 0), 0))
        nxt = pl.BlockSpec((None, None, C_HALO, width),
                           lambda bi, p, i, r=r, nhalo=nhalo: (bi, p, jnp.minimum((i + 1) * r, nhalo - 1), 0))
        o_g, lse_g = pl.pallas_call(
            functools.partial(_attn_dil_kernel, seq=seq, tq=tq),
            grid=(b, dil, seq // tt),
            in_specs=[cur, prev, cur, nxt, prev, cur, nxt],
            out_specs=[cur, pl.BlockSpec((None, None, tt, LANES), lambda bi, p, i: (bi, p, i, 0))],
            out_shape=[jax.ShapeDtypeStruct((b, dil, seq, width), BF16),
                       jax.ShapeDtypeStruct((b, dil, seq, LANES), F32)],
            compiler_params=_params(("parallel", "parallel", "parallel")),
            name=f"attn_dil{g}",
        )(q, k, k, k, v, v, v)
        outs.append(o_g)
        lses.append(lse_g)

    return pl.pallas_call(
        _combine_dil_kernel,
        grid=(b, s // tl),
        in_specs=([phase_spec(dil, width) for _, dil in C_GROUPS]
                  + [phase_spec(dil, LANES) for _, dil in C_GROUPS]),
        out_specs=pl.BlockSpec((None, tl, width), lambda bi, i: (bi, i, 0)),
        out_shape=jax.ShapeDtypeStruct((b, s, width), BF16),
        scratch_shapes=[pltpu.VMEM((len(C_GROUPS) - 1, width // LANES, tl, LANES), F32),
                        pltpu.VMEM((len(C_GROUPS) - 1, tl, LANES), F32)],
        compiler_params=_params(("parallel", "parallel")),
        name="combine_dil",
    )(*outs, *lses)


def _post_kernel(o_ref, x_ref, wo_ref, g_ref, wgu_ref, wout_ref, fg_ref, y_ref, acc_ref, *,
                 apply_final):
    x1 = x_ref[...] + jnp.dot(o_ref[...], wo_ref[...], preferred_element_type=F32)
    h = _rms_rows(x1, g_ref[...]).astype(BF16)
    acc_ref[...] = x1
    for c in range(FFN_DIM // FFN_CHUNK):
        lo = c * FFN_CHUNK
        gate = jnp.dot(h, wgu_ref[:, lo:lo + FFN_CHUNK], preferred_element_type=F32)
        up = jnp.dot(h, wgu_ref[:, FFN_DIM + lo:FFN_DIM + lo + FFN_CHUNK], preferred_element_type=F32)
        act = (gate * jax.nn.sigmoid(gate) * up).astype(BF16)
        acc_ref[...] += jnp.dot(act, wout_ref[lo:lo + FFN_CHUNK, :], preferred_element_type=F32)
    y = acc_ref[...]
    if apply_final:
        y = _rms_rows(y, fg_ref[...])
    y_ref[...] = y


def _post(o, x, w_o, ffn_gain, w_gu, w_out, final_gain, apply_final):
    b, s, d = x.shape
    m = b * s
    tm = _tile(m, 512)
    row = lambda i: (i, 0)
    y = pl.pallas_call(
        functools.partial(_post_kernel, apply_final=apply_final),
        grid=(m // tm,),
        in_specs=[
            pl.BlockSpec((tm, d), row),
            pl.BlockSpec((tm, d), row),
            _const_spec((d, d)),
            _const_spec((1, d)),
            _const_spec((d, 2 * FFN_DIM)),
            _const_spec((FFN_DIM, d)),
            _const_spec((1, d)),
        ],
        out_specs=pl.BlockSpec((tm, d), row),
        out_shape=jax.ShapeDtypeStruct((m, d), F32),
        scratch_shapes=[pltpu.VMEM((tm, d), F32)],
        compiler_params=_params(("parallel",)),
        name="post",
    )(o.reshape(m, d), x.reshape(m, d), w_o.astype(BF16), ffn_gain.reshape(1, d),
      w_gu.astype(BF16), w_out.astype(BF16), final_gain.reshape(1, d))
    return y.reshape(b, s, d)


def kernel(x, attn_norm, ffn_norm, final_norm, a_w_qkv, a_lambda_q1, a_lambda_k1, a_lambda_q2,
           a_lambda_k2, a_subln, a_w_o, b_w_a, b_q_norm, b_kv_norm, b_w_qb, b_w_kvb, b_w_o,
           c_w_qkv, c_w_o, f_w_gu, f_w_out):
    depth = attn_norm.shape[0]
    s = x.shape[1]
    cos_p, sin_p = _rope_tables(s, A_ROT)
    cos_b, sin_b = _rope_tables(s, B_ROPE)
    for i in range(depth):
        mixer, j = i % N_MIXERS, i // N_MIXERS
        if mixer == 0:
            lambda_init = 0.8 - 0.6 * math.exp(-0.3 * i)
            o = _diff_attention(x, attn_norm[i], a_w_qkv[j], a_lambda_q1[j], a_lambda_k1[j],
                                a_lambda_q2[j], a_lambda_k2[j], a_subln[j], lambda_init, cos_p, sin_p)
            w_o = a_w_o[j]
        elif mixer == 1:
            o = _latent_attention(x, attn_norm[i], b_w_a[j], b_q_norm[j], b_kv_norm[j], b_w_qb[j],
                                  b_w_kvb[j], cos_b, sin_b)
            w_o = b_w_o[j]
        else:
            o = _dilated_attention(x, attn_norm[i], c_w_qkv[j], cos_p, sin_p)
            w_o = c_w_o[j]
        x = _post(o, x, w_o, ffn_norm[i], f_w_gu[i], f_w_out[i], final_norm, i == depth - 1)
    return x
```

```python
import functools
import math

import jax
import jax.numpy as jnp
from jax import lax
from jax.experimental import pallas as pl
from jax.experimental.pallas import tpu as pltpu

F32 = jnp.float32
BF16 = jnp.bfloat16

D_MODEL = 1024
N_MIXERS = 3
ROPE_THETA = 500000.0
EPS = 1e-6
NEG_BIG = -1e30
LOG2E = 1.4426950408889634

A_HEADS = 8
A_HEAD_DIM = 64
A_V_DIM = 128
A_ROT = 16

B_HEADS = 16
B_Q_RANK = 384
B_KV_RANK = 256
B_NOPE = 64
B_ROPE = 32
B_V = 64

C_GROUPS = ((128, 1), (512, 4), (2048, 16))
C_HEADS = 16
C_HEAD_DIM = 64
C_ROT = 16
C_HALO = 64

FFN_DIM = 2816
FFN_CHUNK = 256

LANES = 128
BF16_SUBLANES = 16
Q_UNROLL = 2
VMEM_LIMIT_BYTES = 56 * 1024 * 1024

NT_DIMS = (((1,), (1,)), ((), ()))
BOUND_MARGIN = 1.01
MIN_DENOMINATOR = 2.0 ** -60


def _tile(n, pref):
    t = min(n, pref)
    assert n % t == 0, (n, t)
    return t


def _params(sem):
    return pltpu.CompilerParams(dimension_semantics=sem, vmem_limit_bytes=VMEM_LIMIT_BYTES)


def _const_spec(shape):
    nd = len(shape)
    return pl.BlockSpec(shape, lambda *_: (0,) * nd, pipeline_mode=pl.Buffered(1))


def _rms_rows(x, gain):
    return x * lax.rsqrt(jnp.mean(x * x, axis=-1, keepdims=True) + EPS) * gain


def _rope_tables(seq_len, rot_dim):
    pos = jnp.arange(seq_len, dtype=F32)
    inv = ROPE_THETA ** (-jnp.arange(0, rot_dim, 2, dtype=F32) / rot_dim)
    ang = pos[:, None] * inv[None, :]
    return jnp.cos(ang), jnp.sin(ang)


def _lane_tables(cos, sin, period, offset, scale):
    s, r = cos.shape
    c = jnp.ones((s, period), F32)
    c = c.at[:, offset:offset + r].set(cos).at[:, offset + r:offset + 2 * r].set(cos)
    sa = jnp.zeros((s, period), F32).at[:, offset:offset + r].set(-sin)
    sb = jnp.zeros((s, period), F32).at[:, offset + r:offset + 2 * r].set(sin)
    rep = LANES // period
    return jnp.stack([jnp.tile(t, (1, rep)) for t in (c, sa, sb)]) * scale


def _rope(x, tab_ref, base, r):
    return (x * tab_ref[base]
            + pltpu.roll(x, LANES - r, 1) * tab_ref[base + 1]
            + pltpu.roll(x, r, 1) * tab_ref[base + 2])


def _softmax_t(chains, k_ref, vt_ref, tk, shifts=None):
    rows = vt_ref.shape[-2]
    nchunk = vt_ref.shape[-3]
    state = [(jnp.full((1, qm.shape[0]), NEG_BIG, F32), jnp.zeros((1, qm.shape[0]), F32),
              jnp.zeros((rows, qm.shape[0]), F32)) for qm, _, _ in chains]
    items = [(c, n) for c in range(nchunk) for n in range(len(chains))]

    def scores(item):
        c, n = item
        qm, k_lanes, _ = chains[n]
        return lax.dot_general(k_ref[c * tk:(c + 1) * tk, k_lanes], qm, NT_DIMS,
                               preferred_element_type=F32)

    st = scores(items[0])
    for i, (c, n) in enumerate(items):
        st_next = scores(items[i + 1]) if i + 1 < len(items) else None
        m, l, acc = state[n]
        if shifts is None:
            m_new = jnp.maximum(m, jnp.max(st, axis=0, keepdims=True))
            alpha = jnp.exp2(m - m_new)
            p = jnp.exp2(st - m_new)
            l, acc = l * alpha, acc * alpha
        else:
            m_new = m
            p = jnp.exp2(st - shifts[n])
        l = l + jnp.sum(p, axis=0, keepdims=True)
        acc = acc + jnp.dot(vt_ref[chains[n][2] + (c,)], p.astype(BF16), preferred_element_type=F32)
        state[n] = (m_new, l, acc)
        st = st_next
    return [(l, acc) for _, l, acc in state]


def _key_norm_bound(k_ref, k_lanes, sel, width):
    kk = k_ref[:, k_lanes]
    kn2 = lax.dot_general(sel, kk * kk, NT_DIMS, preferred_element_type=F32)
    return jnp.broadcast_to(jnp.sqrt(jnp.max(kn2, axis=-1, keepdims=True)), (8, width))


def _score_bound(qm, kmax):
    ones = jnp.ones((8, qm.shape[1]), BF16)
    qn2 = lax.dot_general(ones, qm * qm, NT_DIMS, preferred_element_type=F32)
    return jnp.sqrt(qn2[0:1]) * kmax[0:1] * BOUND_MARGIN


def _attend_tiles(tile_fn, n_tiles, tq):
    unroll = Q_UNROLL if n_tiles % Q_UNROLL == 0 else 1

    def body(it, carry):
        row0s = [pl.multiple_of((it * unroll + u) * tq, tq) for u in range(unroll)]
        lmins = [tile_fn(r0, False) for r0 in row0s]
        for r0, lmin in zip(row0s, lmins):
            @pl.when(lmin < MIN_DENOMINATOR)
            def _():
                tile_fn(r0, True)
        return carry

    lax.fori_loop(0, n_tiles // unroll, body, 0)


def _vt_chunk_spec(heads, rows, tm, tk):
    if tm >= tk:
        return pl.BlockSpec((None, heads, tm // tk, rows, tk), lambda bi, i: (bi, 0, i, 0, 0))
    per = tk // tm
    return pl.BlockSpec((None, heads, 1, rows, tm), lambda bi, i: (bi, 0, i // per, 0, i % per))


def _proj_diff_kernel(x_ref, g_ref, wqk_ref, wvt_ref, tab_ref, q_ref, k_ref, vt_ref):
    cw = vt_ref.shape[-1]
    h = _rms_rows(x_ref[...], g_ref[...]).astype(BF16)
    for c in range(4):
        y = jnp.dot(h, wqk_ref[:, c * 512:(c + 1) * 512], preferred_element_type=F32)
        dst, base = (q_ref, 0) if c < 2 else (k_ref, 3)
        for j in range(4):
            dst[(c % 2) * 4 + j] = _rope(y[:, j * LANES:(j + 1) * LANES], tab_ref, base,
                                         A_ROT // 2).astype(BF16)
    for half in range(2):
        vt = lax.dot_general(wvt_ref[half * 512:(half + 1) * 512, :], h, NT_DIMS,
                             preferred_element_type=F32)
        for hh in range(4):
            for c in range(vt_ref.shape[1]):
                vt_ref[half * 4 + hh, c] = vt[hh * A_V_DIM:(hh + 1) * A_V_DIM,
                                              c * cw:(c + 1) * cw].astype(BF16)


def _attn_diff_kernel(lam_ref, sub_ref, q_ref, k_ref, vt_ref, o_ref, p_ref, *, tq, tk, lambda_init):
    nck = vt_ref.shape[0]
    low = lax.broadcasted_iota(jnp.int32, (tq, LANES), 1) < A_HEAD_DIM
    sel_low = lax.broadcasted_iota(jnp.int32, (8, LANES), 1) < A_HEAD_DIM
    kmax = [_key_norm_bound(k_ref, slice(None), sel.astype(BF16), tq)
            for sel in (sel_low, jnp.logical_not(sel_low))]
    l1 = jnp.sum(lam_ref[0:1, :] * lam_ref[1:2, :], axis=-1, keepdims=True)
    l2 = jnp.sum(lam_ref[2:3, :] * lam_ref[3:4, :], axis=-1, keepdims=True)
    lam = jnp.exp(l1) - jnp.exp(l2) + lambda_init
    items = [(c, n) for c in range(nck) for n in range(2)]

    def tile(row0, exact):
        q2 = q_ref[pl.ds(row0, tq), :]
        zero = jnp.zeros_like(q2)
        qms = (jnp.where(low, q2, zero), jnp.where(low, zero, q2))

        def scores(item):
            c, n = item
            return lax.dot_general(k_ref[c * tk:(c + 1) * tk, :], qms[n], NT_DIMS,
                                   preferred_element_type=F32)

        if exact:
            shifts = [jnp.full((1, tq), NEG_BIG, F32), jnp.full((1, tq), NEG_BIG, F32)]
            for c, n in items:
                shifts[n] = jnp.maximum(shifts[n], jnp.max(scores((c, n)), axis=0, keepdims=True))
        else:
            shifts = [_score_bound(qms[n], kmax[n]) for n in range(2)]
        dens = [jnp.zeros((1, tq), F32), jnp.zeros((1, tq), F32)]
        st = scores(items[0])
        for i, (c, n) in enumerate(items):
            st_next = scores(items[i + 1]) if i + 1 < len(items) else None
            p = jnp.exp2(st - shifts[n])
            p_ref[n, c] = p.astype(BF16)
            dens[n] = dens[n] + jnp.sum(p, axis=0, keepdims=True)
            st = st_next
        la, lb = dens
        ratio = (lam * la / lb).astype(BF16)
        acc = jnp.zeros((A_V_DIM, tq), F32)
        for c in range(nck):
            pt = p_ref[0, c] - p_ref[1, c] * ratio
            acc = acc + jnp.dot(vt_ref[c], pt, preferred_element_type=F32)
        o = (acc * (1.0 / la)).T
        o_ref[pl.ds(row0, tq), :] = (_rms_rows(o, sub_ref[...]) * (1.0 - lambda_init)).astype(o_ref.dtype)
        return jnp.min(jnp.minimum(la, lb))

    _attend_tiles(tile, q_ref.shape[0] // tq, tq)


def _diff_attention(x, gain, w_qkv, lam_q1, lam_k1, lam_q2, lam_k2, subln, lambda_init, cos, sin):
    b, s, d = x.shape
    tq = _tile(s, 512)
    tk = _tile(s, 1024)
    tm = _tile(s, 1024)
    nck = s // tk
    qk_w = 2 * A_HEADS * A_HEAD_DIM
    scale = A_HEAD_DIM ** -0.5 * LOG2E
    tab = jnp.concatenate([_lane_tables(cos, sin, A_HEAD_DIM, 0, scale),
                           _lane_tables(cos, sin, A_HEAD_DIM, 0, 1.0)])
    wqk = w_qkv[:, :2 * qk_w].astype(BF16)
    wvt = w_qkv[:, 2 * qk_w:].T.astype(BF16)

    head_major = pl.BlockSpec((None, A_HEADS, tm, LANES), lambda bi, i: (bi, 0, i, 0))
    q, k, vt = pl.pallas_call(
        _proj_diff_kernel,
        grid=(b, s // tm),
        in_specs=[
            pl.BlockSpec((None, tm, d), lambda bi, i: (bi, i, 0)),
            _const_spec((1, d)),
            _const_spec((d, 2 * qk_w)),
            _const_spec((A_HEADS * A_V_DIM, d)),
            pl.BlockSpec((6, tm, LANES), lambda bi, i: (0, i, 0)),
        ],
        out_specs=[
            head_major,
            head_major,
            _vt_chunk_spec(A_HEADS, A_V_DIM, tm, tk),
        ],
        out_shape=[
            jax.ShapeDtypeStruct((b, A_HEADS, s, LANES), BF16),
            jax.ShapeDtypeStruct((b, A_HEADS, s, LANES), BF16),
            jax.ShapeDtypeStruct((b, A_HEADS, nck, A_V_DIM, tk), BF16),
        ],
        compiler_params=_params(("parallel", "parallel")),
        name="proj_diff",
    )(x, gain.reshape(1, d), wqk, wvt, tab)

    lam = jnp.stack([lam_q1, lam_k1, lam_q2, lam_k2]).astype(F32)
    per_head = pl.BlockSpec((None, None, s, LANES), lambda bi, h: (bi, h, 0, 0))
    return pl.pallas_call(
        functools.partial(_attn_diff_kernel, tq=tq, tk=tk, lambda_init=lambda_init),
        grid=(b, A_HEADS),
        in_specs=[
            _const_spec((4, A_HEAD_DIM)),
            _const_spec((1, A_V_DIM)),
            per_head,
            per_head,
            pl.BlockSpec((None, None, nck, A_V_DIM, tk), lambda bi, h: (bi, h, 0, 0, 0)),
        ],
        out_specs=pl.BlockSpec((None, s, A_V_DIM), lambda bi, h: (bi, 0, h)),
        out_shape=jax.ShapeDtypeStruct((b, s, A_HEADS * A_V_DIM), BF16),
        scratch_shapes=[pltpu.VMEM((2, nck, tk, tq), BF16)],
        compiler_params=_params(("parallel", "parallel")),
        name="attn_diff",
    )(lam, subln.reshape(1, A_V_DIM).astype(F32), q, k, vt)


def _proj_mla_kernel(x_ref, g_ref, waq_ref, wakv_ref, war_ref, qn_ref, kvn_ref, wqb_ref, wkb_ref,
                     wvt_ref, tab_ref, q_ref, k_ref, vt_ref):
    cw = vt_ref.shape[-1]
    r = B_ROPE // 2
    h = _rms_rows(x_ref[...], g_ref[...]).astype(BF16)
    q_lat = _rms_rows(jnp.dot(h, waq_ref[...], preferred_element_type=F32), qn_ref[...]).astype(BF16)
    kv_lat = _rms_rows(jnp.dot(h, wakv_ref[...], preferred_element_type=F32), kvn_ref[...]).astype(BF16)
    k_rope = _rope(jnp.dot(h, war_ref[...], preferred_element_type=F32), tab_ref, 3, r)
    for c in range(4):
        yq = jnp.dot(q_lat, wqb_ref[:, c * 512:(c + 1) * 512], preferred_element_type=F32)
        yk = jnp.dot(kv_lat, wkb_ref[:, c * 512:(c + 1) * 512], preferred_element_type=F32)
        for j in range(4):
            pair, lanes = 2 * c + j // 2, slice((j % 2) * LANES, (j % 2 + 1) * LANES)
            q_ref[pair, :, lanes] = _rope(yq[:, j * LANES:(j + 1) * LANES], tab_ref, 0, r).astype(BF16)
            k_ref[pair, :, lanes] = (yk[:, j * LANES:(j + 1) * LANES] + k_rope).astype(BF16)
    for half in range(2):
        vt = lax.dot_general(wvt_ref[half * 512:(half + 1) * 512, :], kv_lat, NT_DIMS,
                             preferred_element_type=F32)
        for hp in range(4):
            for c in range(vt_ref.shape[1]):
                vt_ref[half * 4 + hp, c] = vt[hp * LANES:(hp + 1) * LANES, c * cw:(c + 1) * cw].astype(BF16)


def _attn_mla_kernel(q_ref, k_ref, vt_ref, o_ref, *, tq, tk):
    lanes = [slice(e * LANES, (e + 1) * LANES) for e in range(2)]
    ones = jnp.ones((8, LANES), BF16)
    kmax = [_key_norm_bound(k_ref, lanes[e], ones, tq) for e in range(2)]

    def tile(row0, exact):
        chains = [(q_ref[pl.ds(row0, tq), lanes[e]], lanes[e], ()) for e in range(2)]
        shifts = None if exact else [_score_bound(chains[n][0], kmax[n]) for n in range(2)]
        (la, acca), (lb, accb) = _softmax_t(chains, k_ref, vt_ref, tk, shifts)
        o = jnp.concatenate([acca[0:B_V] * (1.0 / la), accb[B_V:2 * B_V] * (1.0 / lb)], axis=0).T
        o_ref[pl.ds(row0, tq), :] = o.astype(o_ref.dtype)
        return jnp.min(jnp.minimum(la, lb))

    _attend_tiles(tile, q_ref.shape[0] // tq, tq)


def _latent_attention(x, gain, w_a, q_norm, kv_norm, w_qb, w_kvb, cos, sin):
    b, s, d = x.shape
    tq = _tile(s, 512)
    tk = _tile(s, 4096)
    tm = _tile(s, 1024)
    nck = s // tk
    hw = B_HEADS * LANES
    npair = B_HEADS // 2
    scale = (B_NOPE + B_ROPE) ** -0.5 * LOG2E
    tab = jnp.concatenate([_lane_tables(cos, sin, LANES, B_NOPE, scale),
                           _lane_tables(cos, sin, LANES, B_NOPE, 1.0)])
    waq = w_a[:, :B_Q_RANK].astype(BF16)
    wakv = w_a[:, B_Q_RANK:B_Q_RANK + B_KV_RANK].astype(BF16)
    war = jnp.zeros((d, LANES), F32).at[:, B_NOPE:B_NOPE + B_ROPE].set(
        w_a[:, B_Q_RANK + B_KV_RANK:]).astype(BF16)
    pad = LANES - (B_NOPE + B_ROPE)
    wqb = jnp.pad(w_qb.reshape(B_Q_RANK, B_HEADS, B_NOPE + B_ROPE),
                  ((0, 0), (0, 0), (0, pad))).reshape(B_Q_RANK, hw).astype(BF16)
    wkv3 = w_kvb.reshape(B_KV_RANK, B_HEADS, B_NOPE + B_V)
    wkb = jnp.pad(wkv3[:, :, :B_NOPE], ((0, 0), (0, 0), (0, LANES - B_NOPE))
                  ).reshape(B_KV_RANK, hw).astype(BF16)
    wvt = wkv3[:, :, B_NOPE:].reshape(B_KV_RANK, B_HEADS * B_V).T.astype(BF16)

    pair_major = pl.BlockSpec((None, npair, tm, 2 * LANES), lambda bi, i: (bi, 0, i, 0))
    q, k, vt = pl.pallas_call(
        _proj_mla_kernel,
        grid=(b, s // tm),
        in_specs=[
            pl.BlockSpec((None, tm, d), lambda bi, i: (bi, i, 0)),
            _const_spec((1, d)),
            _const_spec((d, B_Q_RANK)),
            _const_spec((d, B_KV_RANK)),
            _const_spec((d, LANES)),
            _const_spec((1, B_Q_RANK)),
            _const_spec((1, B_KV_RANK)),
            _const_spec((B_Q_RANK, hw)),
            _const_spec((B_KV_RANK, hw)),
            _const_spec((B_HEADS * B_V, B_KV_RANK)),
            pl.BlockSpec((6, tm, LANES), lambda bi, i: (0, i, 0)),
        ],
        out_specs=[
            pair_major,
            pair_major,
            _vt_chunk_spec(npair, 2 * B_V, tm, tk),
        ],
        out_shape=[
            jax.ShapeDtypeStruct((b, npair, s, 2 * LANES), BF16),
            jax.ShapeDtypeStruct((b, npair, s, 2 * LANES), BF16),
            jax.ShapeDtypeStruct((b, npair, nck, 2 * B_V, tk), BF16),
        ],
        compiler_params=_params(("parallel", "parallel")),
        name="proj_mla",
    )(x, gain.reshape(1, d), waq, wakv, war, q_norm.reshape(1, -1), kv_norm.reshape(1, -1),
      wqb, wkb, wvt, tab)

    per_pair = pl.BlockSpec((None, None, s, 2 * LANES), lambda bi, h: (bi, h, 0, 0))
    return pl.pallas_call(
        functools.partial(_attn_mla_kernel, tq=tq, tk=tk),
        grid=(b, npair),
        in_specs=[
            per_pair,
            per_pair,
            pl.BlockSpec((None, None, nck, 2 * B_V, tk), lambda bi, h: (bi, h, 0, 0, 0)),
        ],
        out_specs=pl.BlockSpec((None, s, 2 * B_V), lambda bi, h: (bi, 0, h)),
        out_shape=jax.ShapeDtypeStruct((b, s, B_HEADS * B_V), BF16),
        compiler_params=_params(("parallel", "parallel")),
        name="attn_mla",
    )(q, k, vt)


def _proj_dil_kernel(x_ref, g_ref, w_ref, tab_ref, *refs):
    outs, scr = refs[:-1], refs[-1]
    tl = x_ref.shape[0]
    h = _rms_rows(x_ref[...], g_ref[...]).astype(BF16)
    width = C_HEADS * C_HEAD_DIM
    for g, (_, dil) in enumerate(C_GROUPS):
        for c in range(3 * width // 512):
            col = g * 3 * width + c * 512
            y = jnp.dot(h, w_ref[:, col:col + 512], preferred_element_type=F32)
            which = c // 2
            dst = outs[3 * g + which]
            lo = (c % 2) * 512
            for j in range(4):
                yj = y[:, j * LANES:(j + 1) * LANES]
                if which < 2:
                    yj = _rope(yj, tab_ref, 3 * which, C_ROT // 2)
                cols = slice(lo + j * LANES, lo + (j + 1) * LANES)
                if dil == 1:
                    dst[0, :, cols] = yj.astype(BF16)
                    continue
                scr[j] = yj
                for p in range(dil):
                    dst[p, :, cols] = scr[j, pl.ds(p, tl // dil, stride=dil), :].astype(BF16)


def _attn_dil_kernel(q_ref, kp_ref, kc_ref, kn_ref, vp_ref, vc_ref, vn_ref, o_ref, lse_ref, *, seq, tq):
    tt = q_ref.shape[0]
    nsub = tt // tq
    w = tq + 2 * C_HALO
    t0 = pl.program_id(2) * tt
    col = lax.broadcasted_iota(jnp.int32, (tq, w), 1)
    band = jnp.abs(col - C_HALO - lax.broadcasted_iota(jnp.int32, (tq, w), 0)) <= C_HALO
    head_lane = lax.broadcasted_iota(jnp.int32, (tq, LANES), 1)
    low = head_lane < C_HEAD_DIM
    cache = {}

    def window(prev_ref, cur_ref, next_ref, sub, lanes):
        lo, hi = sub * tq - C_HALO, (sub + 1) * tq + C_HALO
        parts = []
        if lo < 0:
            parts.append(prev_ref[:, lanes])
        parts.append(cur_ref[max(lo, 0):min(hi, tt), lanes])
        if hi > tt:
            parts.append(next_ref[:, lanes])
        return parts[0] if len(parts) == 1 else jnp.concatenate(parts, axis=0)

    def operands(sub, j):
        if (sub, j) not in cache:
            lanes = slice(j * LANES, (j + 1) * LANES)
            cache[(sub, j)] = (q_ref[sub * tq:(sub + 1) * tq, lanes],
                               window(kp_ref, kc_ref, kn_ref, sub, lanes),
                               window(vp_ref, vc_ref, vn_ref, sub, lanes))
        return cache[(sub, j)]

    def mask(sub):
        if ("mask", sub) not in cache:
            kpos = t0 + sub * tq - C_HALO + col
            cache[("mask", sub)] = band & (kpos >= 0) & (kpos < seq)
        return cache[("mask", sub)]

    def scores(unit):
        sub, j = unit
        q2 = operands(sub, j)[0]
        zero = jnp.zeros_like(q2)
        qm = jnp.concatenate([jnp.where(low, q2, zero), jnp.where(low, zero, q2)], axis=0)
        return lax.dot_general(qm, operands(sub, j)[1], NT_DIMS, preferred_element_type=F32)

    units = [(sub, j) for sub in range(nsub) for j in range(C_HEADS // 2)]
    sc = scores(units[0])
    lse_all = None
    for i, (sub, j) in enumerate(units):
        sc_next = scores(units[i + 1]) if i + 1 < len(units) else None
        msk = mask(sub)
        sc = jnp.where(jnp.concatenate([msk, msk], axis=0), sc, NEG_BIG)
        m = jnp.max(sc, axis=-1, keepdims=True)
        p = jnp.exp2(sc - m)
        l = jnp.sum(p, axis=-1, keepdims=True)
        pv = jnp.dot(p.astype(BF16), operands(sub, j)[2], preferred_element_type=F32) * (1.0 / l)
        lse = m + jnp.log2(l)
        rows = slice(sub * tq, (sub + 1) * tq)
        o_ref[rows, j * LANES:(j + 1) * LANES] = jnp.where(low, pv[0:tq], pv[tq:2 * tq]).astype(o_ref.dtype)
        for e in range(2):
            lse_e = lse[e * tq:(e + 1) * tq]
            lse_all = lse_e + jnp.zeros((tq, LANES), F32) if (j, e) == (0, 0) else jnp.where(
                head_lane == 2 * j + e, lse_e, lse_all)
        if j == C_HEADS // 2 - 1:
            lse_ref[rows, :] = lse_all
        sc = sc_next


def _combine_dil_kernel(o1_ref, o2_ref, o3_ref, l1_ref, l2_ref, l3_ref, o_ref, so_ref, sl_ref):
    rows = o_ref.shape[0]
    nlg = o_ref.shape[1] // LANES
    os_, ls = [], []
    for g, ((_, dil), oref, lref) in enumerate(zip(C_GROUPS, (o1_ref, o2_ref, o3_ref),
                                                   (l1_ref, l2_ref, l3_ref))):
        if dil == 1:
            os_.append(oref[0].astype(F32))
            ls.append(lref[0])
            continue
        for p in range(dil):
            og = oref[p].astype(F32)
            for j in range(nlg):
                so_ref[g - 1, j, pl.ds(p, rows // dil, stride=dil), :] = og[:, j * LANES:(j + 1) * LANES]
            sl_ref[g - 1, pl.ds(p, rows // dil, stride=dil), :] = lref[p]
        os_.append(jnp.concatenate([so_ref[g - 1, j] for j in range(nlg)], axis=-1))
        ls.append(sl_ref[g - 1])
    m = jnp.maximum(jnp.maximum(ls[0], ls[1]), ls[2])
    es = [jnp.exp2(l - m) for l in ls]
    inv = 1.0 / (es[0] + es[1] + es[2])
    width = C_HEADS * C_HEAD_DIM
    expand = (lax.broadcasted_iota(jnp.int32, (LANES, width), 1) // C_HEAD_DIM
              == lax.broadcasted_iota(jnp.int32, (LANES, width), 0)).astype(BF16)
    acc = jnp.zeros(o_ref.shape, F32)
    for e, og in zip(es, os_):
        wt = e * inv
        hi = wt.astype(BF16)
        lo = (wt - hi.astype(F32)).astype(BF16)
        wide = (jnp.dot(hi, expand, preferred_element_type=F32)
                + jnp.dot(lo, expand, preferred_element_type=F32))
        acc = acc + wide * og
    o_ref[...] = acc.astype(o_ref.dtype)


def _dilated_attention(x, gain, w_qkv, cos, sin):
    b, s, d = x.shape
    width = C_HEADS * C_HEAD_DIM
    scale = C_HEAD_DIM ** -0.5 * LOG2E
    tab = jnp.concatenate([_lane_tables(cos, sin, C_HEAD_DIM, 0, scale),
                           _lane_tables(cos, sin, C_HEAD_DIM, 0, 1.0)])
    dmax = max(dil for _, dil in C_GROUPS)
    for window, dil in C_GROUPS:
        assert (window // 2) // dil == C_HALO and (s // dil) % C_HALO == 0

    tl = _tile(s, 512)
    assert tl % (BF16_SUBLANES * dmax) == 0
    phase_spec = lambda dil, cols: pl.BlockSpec((None, dil, tl // dil, cols),
                                                lambda bi, i: (bi, 0, i, 0))
    qkv = pl.pallas_call(
        _proj_dil_kernel,
        grid=(b, s // tl),
        in_specs=[
            pl.BlockSpec((None, tl, d), lambda bi, i: (bi, i, 0)),
            _const_spec((1, d)),
            _const_spec((d, len(C_GROUPS) * 3 * width)),
            pl.BlockSpec((6, tl, LANES), lambda bi, i: (0, i, 0)),
        ],
        out_specs=[phase_spec(dil, width) for _, dil in C_GROUPS for _ in range(3)],
        out_shape=[jax.ShapeDtypeStruct((b, dil, s // dil, width), BF16)
                   for _, dil in C_GROUPS for _ in range(3)],
        scratch_shapes=[pltpu.VMEM((4, tl, LANES), F32)],
        compiler_params=_params(("parallel", "parallel")),
        name="proj_dil",
    )(x, gain.reshape(1, d), w_qkv.astype(BF16), tab)

    outs, lses = [], []
    for g, (_, dil) in enumerate(C_GROUPS):
        q, k, v = qkv[3 * g:3 * g + 3]
        seq = s // dil
        tt = _tile(seq, 512)
        tq = _tile(tt, 128)
        r = tt // C_HALO
        nhalo = seq // C_HALO
        cur = pl.BlockSpec((None, None, tt, width), lambda bi, p, i: (bi, p, i, 0))
        prev = pl.BlockSpec((None, None, C_HALO, width),
                            lambda bi, p, i, r=r: (bi, p, jnp.maximum(i * r - 1, 0), 0))
        nxt = pl.BlockSpec((None, None, C_HALO, width),
                           lambda bi, p, i, r=r, nhalo=nhalo: (bi, p, jnp.minimum((i + 1) * r, nhalo - 1), 0))
        o_g, lse_g = pl.pallas_call(
            functools.partial(_attn_dil_kernel, seq=seq, tq=tq),
            grid=(b, dil, seq // tt),
            in_specs=[cur, prev, cur, nxt, prev, cur, nxt],
            out_specs=[cur, pl.BlockSpec((None, None, tt, LANES), lambda bi, p, i: (bi, p, i, 0))],
            out_shape=[jax.ShapeDtypeStruct((b, dil, seq, width), BF16),
                       jax.ShapeDtypeStruct((b, dil, seq, LANES), F32)],
            compiler_params=_params(("parallel", "parallel", "parallel")),
            name=f"attn_dil{g}",
        )(q, k, k, k, v, v, v)
        outs.append(o_g)
        lses.append(lse_g)

    return pl.pallas_call(
        _combine_dil_kernel,
        grid=(b, s // tl),
        in_specs=([phase_spec(dil, width) for _, dil in C_GROUPS]
                  + [phase_spec(dil, LANES) for _, dil in C_GROUPS]),
        out_specs=pl.BlockSpec((None, tl, width), lambda bi, i: (bi, i, 0)),
        out_shape=jax.ShapeDtypeStruct((b, s, width), BF16),
        scratch_shapes=[pltpu.VMEM((len(C_GROUPS) - 1, width // LANES, tl, LANES), F32),
                        pltpu.VMEM((len(C_GROUPS) - 1, tl, LANES), F32)],
        compiler_params=_params(("parallel", "parallel")),
        name="combine_dil",
    )(*outs, *lses)


def _post_kernel(o_ref, x_ref, wo_ref, g_ref, wgu_ref, wout_ref, fg_ref, y_ref, acc_ref, *,
                 apply_final):
    x1 = x_ref[...] + jnp.dot(o_ref[...], wo_ref[...], preferred_element_type=F32)
    h = _rms_rows(x1, g_ref[...]).astype(BF16)
    acc_ref[...] = x1
    for c in range(FFN_DIM // FFN_CHUNK):
        lo = c * FFN_CHUNK
        gate = jnp.dot(h, wgu_ref[:, lo:lo + FFN_CHUNK], preferred_element_type=F32)
        up = jnp.dot(h, wgu_ref[:, FFN_DIM + lo:FFN_DIM + lo + FFN_CHUNK], preferred_element_type=F32)
        act = (gate * jax.nn.sigmoid(gate) * up).astype(BF16)
        acc_ref[...] += jnp.dot(act, wout_ref[lo:lo + FFN_CHUNK, :], preferred_element_type=F32)
    y = acc_ref[...]
    if apply_final:
        y = _rms_rows(y, fg_ref[...])
    y_ref[...] = y


def _post(o, x, w_o, ffn_gain, w_gu, w_out, final_gain, apply_final):
    b, s, d = x.shape
    m = b * s
    tm = _tile(m, 512)
    row = lambda i: (i, 0)
    y = pl.pallas_call(
        functools.partial(_post_kernel, apply_final=apply_final),
        grid=(m // tm,),
        in_specs=[
            pl.BlockSpec((tm, d), row),
            pl.BlockSpec((tm, d), row),
            _const_spec((d, d)),
            _const_spec((1, d)),
            _const_spec((d, 2 * FFN_DIM)),
            _const_spec((FFN_DIM, d)),
            _const_spec((1, d)),
        ],
        out_specs=pl.BlockSpec((tm, d), row),
        out_shape=jax.ShapeDtypeStruct((m, d), F32),
        scratch_shapes=[pltpu.VMEM((tm, d), F32)],
        compiler_params=_params(("parallel",)),
        name="post",
    )(o.reshape(m, d), x.reshape(m, d), w_o.astype(BF16), ffn_gain.reshape(1, d),
      w_gu.astype(BF16), w_out.astype(BF16), final_gain.reshape(1, d))
    return y.reshape(b, s, d)


def kernel(x, attn_norm, ffn_norm, final_norm, a_w_qkv, a_lambda_q1, a_lambda_k1, a_lambda_q2,
           a_lambda_k2, a_subln, a_w_o, b_w_a, b_q_norm, b_kv_norm, b_w_qb, b_w_kvb, b_w_o,
           c_w_qkv, c_w_o, f_w_gu, f_w_out):
    depth = attn_norm.shape[0]
    s = x.shape[1]
    cos_p, sin_p = _rope_tables(s, A_ROT)
    cos_b, sin_b = _rope_tables(s, B_ROPE)
    for i in range(depth):
        mixer, j = i % N_MIXERS, i // N_MIXERS
        if mixer == 0:
            lambda_init = 0.8 - 0.6 * math.exp(-0.3 * i)
            o = _diff_attention(x, attn_norm[i], a_w_qkv[j], a_lambda_q1[j], a_lambda_k1[j],
                                a_lambda_q2[j], a_lambda_k2[j], a_subln[j], lambda_init, cos_p, sin_p)
            w_o = a_w_o[j]
        elif mixer == 1:
            o = _latent_attention(x, attn_norm[i], b_w_a[j], b_q_norm[j], b_kv_norm[j], b_w_qb[j],
                                  b_w_kvb[j], cos_b, sin_b)
            w_o = b_w_o[j]
        else:
            o = _dilated_attention(x, attn_norm[i], c_w_qkv[j], cos_p, sin_p)
            w_o = c_w_o[j]
        x = _post(o, x, w_o, ffn_norm[i], f_w_gu[i], f_w_out[i], final_norm, i == depth - 1)
    return x
```

```python
import functools
import math

import jax
import jax.numpy as jnp
from jax import lax
from jax.experimental import pallas as pl
from jax.experimental.pallas import tpu as pltpu

F32 = jnp.float32
BF16 = jnp.bfloat16

D_MODEL = 1024
N_MIXERS = 3
ROPE_THETA = 500000.0
EPS = 1e-6
NEG_BIG = -1e30
LOG2E = 1.4426950408889634

A_HEADS = 8
A_HEAD_DIM = 64
A_V_DIM = 128
A_ROT = 16

B_HEADS = 16
B_Q_RANK = 384
B_KV_RANK = 256
B_NOPE = 64
B_ROPE = 32
B_V = 64

C_GROUPS = ((128, 1), (512, 4), (2048, 16))
C_HEADS = 16
C_HEAD_DIM = 64
C_ROT = 16
C_HALO = 64

FFN_DIM = 2816
FFN_CHUNK = 256

LANES = 128
BF16_SUBLANES = 16
Q_UNROLL = 2
VMEM_LIMIT_BYTES = 56 * 1024 * 1024

NT_DIMS = (((1,), (1,)), ((), ()))
BOUND_MARGIN = 1.01
MIN_DENOMINATOR = 2.0 ** -60


def _tile(n, pref):
    t = min(n, pref)
    assert n % t == 0, (n, t)
    return t


def _params(sem):
    return pltpu.CompilerParams(dimension_semantics=sem, vmem_limit_bytes=VMEM_LIMIT_BYTES)


def _const_spec(shape):
    nd = len(shape)
    return pl.BlockSpec(shape, lambda *_: (0,) * nd, pipeline_mode=pl.Buffered(1))


def _rms_rows(x, gain):
    return x * lax.rsqrt(jnp.mean(x * x, axis=-1, keepdims=True) + EPS) * gain


def _rope_tables(seq_len, rot_dim):
    pos = jnp.arange(seq_len, dtype=F32)
    inv = ROPE_THETA ** (-jnp.arange(0, rot_dim, 2, dtype=F32) / rot_dim)
    ang = pos[:, None] * inv[None, :]
    return jnp.cos(ang), jnp.sin(ang)


def _lane_tables(cos, sin, period, offset, scale):
    s, r = cos.shape
    c = jnp.ones((s, period), F32)
    c = c.at[:, offset:offset + r].set(cos).at[:, offset + r:offset + 2 * r].set(cos)
    sa = jnp.zeros((s, period), F32).at[:, offset:offset + r].set(-sin)
    sb = jnp.zeros((s, period), F32).at[:, offset + r:offset + 2 * r].set(sin)
    rep = LANES // period
    return jnp.stack([jnp.tile(t, (1, rep)) for t in (c, sa, sb)]) * scale


def _rope(x, tab_ref, base, r):
    return (x * tab_ref[base]
            + pltpu.roll(x, LANES - r, 1) * tab_ref[base + 1]
            + pltpu.roll(x, r, 1) * tab_ref[base + 2])


def _softmax_t(chains, k_ref, vt_ref, tk, shifts=None):
    rows = vt_ref.shape[-2]
    nchunk = vt_ref.shape[-3]
    state = [(jnp.full((1, qm.shape[0]), NEG_BIG, F32), jnp.zeros((1, qm.shape[0]), F32),
              jnp.zeros((rows, qm.shape[0]), F32)) for qm, _, _ in chains]
    items = [(c, n) for c in range(nchunk) for n in range(len(chains))]

    def scores(item):
        c, n = item
        qm, k_lanes, _ = chains[n]
        return lax.dot_general(k_ref[c * tk:(c + 1) * tk, k_lanes], qm, NT_DIMS,
                               preferred_element_type=F32)

    st = scores(items[0])
    for i, (c, n) in enumerate(items):
        st_next = scores(items[i + 1]) if i + 1 < len(items) else None
        m, l, acc = state[n]
        if shifts is None:
            m_new = jnp.maximum(m, jnp.max(st, axis=0, keepdims=True))
            alpha = jnp.exp2(m - m_new)
            p = jnp.exp2(st - m_new)
            l, acc = l * alpha, acc * alpha
        else:
            m_new = m
            p = jnp.exp2(st - shifts[n])
        l = l + jnp.sum(p, axis=0, keepdims=True)
        acc = acc + jnp.dot(vt_ref[chains[n][2] + (c,)], p.astype(BF16), preferred_element_type=F32)
        state[n] = (m_new, l, acc)
        st = st_next
    return [(l, acc) for _, l, acc in state]


def _key_norm_bound(k_ref, k_lanes, sel, width):
    kk = k_ref[:, k_lanes]
    kn2 = lax.dot_general(sel, kk * kk, NT_DIMS, preferred_element_type=F32)
    return jnp.broadcast_to(jnp.sqrt(jnp.max(kn2, axis=-1, keepdims=True)), (8, width))


def _score_bound(qm, kmax):
    ones = jnp.ones((8, qm.shape[1]), BF16)
    qn2 = lax.dot_general(ones, qm * qm, NT_DIMS, preferred_element_type=F32)
    return jnp.sqrt(qn2[0:1]) * kmax[0:1] * BOUND_MARGIN


def _attend_tiles(tile_fn, n_tiles, tq):
    unroll = Q_UNROLL if n_tiles % Q_UNROLL == 0 else 1

    def body(it, carry):
        row0s = [pl.multiple_of((it * unroll + u) * tq, tq) for u in range(unroll)]
        lmins = [tile_fn(r0, False) for r0 in row0s]
        for r0, lmin in zip(row0s, lmins):
            @pl.when(lmin < MIN_DENOMINATOR)
            def _():
                tile_fn(r0, True)
        return carry

    lax.fori_loop(0, n_tiles // unroll, body, 0)


def _vt_chunk_spec(heads, rows, tm, tk):
    if tm >= tk:
        return pl.BlockSpec((None, heads, tm // tk, rows, tk), lambda bi, i: (bi, 0, i, 0, 0))
    per = tk // tm
    return pl.BlockSpec((None, heads, 1, rows, tm), lambda bi, i: (bi, 0, i // per, 0, i % per))


def _proj_diff_kernel(x_ref, g_ref, wqk_ref, wvt_ref, tab_ref, q_ref, k_ref, vt_ref):
    cw = vt_ref.shape[-1]
    h = _rms_rows(x_ref[...], g_ref[...]).astype(BF16)
    for c in range(4):
        y = jnp.dot(h, wqk_ref[:, c * 512:(c + 1) * 512], preferred_element_type=F32)
        dst, base = (q_ref, 0) if c < 2 else (k_ref, 3)
        for j in range(4):
            dst[(c % 2) * 4 + j] = _rope(y[:, j * LANES:(j + 1) * LANES], tab_ref, base,
                                         A_ROT // 2).astype(BF16)
    for half in range(2):
        vt = lax.dot_general(wvt_ref[half * 512:(half + 1) * 512, :], h, NT_DIMS,
                             preferred_element_type=F32)
        for hh in range(4):
            for c in range(vt_ref.shape[1]):
                vt_ref[half * 4 + hh, c] = vt[hh * A_V_DIM:(hh + 1) * A_V_DIM,
                                              c * cw:(c + 1) * cw].astype(BF16)


def _attn_diff_kernel(lam_ref, sub_ref, q_ref, k_ref, vt_ref, o_ref, p_ref, *, tq, tk, lambda_init):
    nck = vt_ref.shape[0]
    low = lax.broadcasted_iota(jnp.int32, (tq, LANES), 1) < A_HEAD_DIM
    sel_low = lax.broadcasted_iota(jnp.int32, (8, LANES), 1) < A_HEAD_DIM
    kmax = [_key_norm_bound(k_ref, slice(None), sel.astype(BF16), tq)
            for sel in (sel_low, jnp.logical_not(sel_low))]
    l1 = jnp.sum(lam_ref[0:1, :] * lam_ref[1:2, :], axis=-1, keepdims=True)
    l2 = jnp.sum(lam_ref[2:3, :] * lam_ref[3:4, :], axis=-1, keepdims=True)
    lam = jnp.exp(l1) - jnp.exp(l2) + lambda_init
    items = [(c, n) for c in range(nck) for n in range(2)]

    def tile(row0, exact):
        q2 = q_ref[pl.ds(row0, tq), :]
        zero = jnp.zeros_like(q2)
        qms = (jnp.where(low, q2, zero), jnp.where(low, zero, q2))

        def scores(item):
            c, n = item
            return lax.dot_general(k_ref[c * tk:(c + 1) * tk, :], qms[n], NT_DIMS,
                                   preferred_element_type=F32)

        if exact:
            shifts = [jnp.full((1, tq), NEG_BIG, F32), jnp.full((1, tq), NEG_BIG, F32)]
            for c, n in items:
                shifts[n] = jnp.maximum(shifts[n], jnp.max(scores((c, n)), axis=0, keepdims=True))
        else:
            shifts = [_score_bound(qms[n], kmax[n]) for n in range(2)]
        dens = [jnp.zeros((1, tq), F32), jnp.zeros((1, tq), F32)]
        st = scores(items[0])
        for i, (c, n) in enumerate(items):
            st_next = scores(items[i + 1]) if i + 1 < len(items) else None
            p = jnp.exp2(st - shifts[n])
            p_ref[n, c] = p.astype(BF16)
            dens[n] = dens[n] + jnp.sum(p, axis=0, keepdims=True)
            st = st_next
        la, lb = dens
        ratio = (lam * la / lb).astype(BF16)
        acc = jnp.zeros((A_V_DIM, tq), F32)
        for c in range(nck):
            pt = p_ref[0, c] - p_ref[1, c] * ratio
            acc = acc + jnp.dot(vt_ref[c], pt, preferred_element_type=F32)
        o = (acc * (1.0 / la)).T
        o_ref[pl.ds(row0, tq), :] = (_rms_rows(o, sub_ref[...]) * (1.0 - lambda_init)).astype(o_ref.dtype)
        return jnp.min(jnp.minimum(la, lb))

    _attend_tiles(tile, q_ref.shape[0] // tq, tq)


def _diff_attention(x, gain, w_qkv, lam_q1, lam_k1, lam_q2, lam_k2, subln, lambda_init, cos, sin):
    b, s, d = x.shape
    tq = _tile(s, 512)
    tk = _tile(s, 1024)
    tm = _tile(s, 1024)
    nck = s // tk
    qk_w = 2 * A_HEADS * A_HEAD_DIM
    scale = A_HEAD_DIM ** -0.5 * LOG2E
    tab = jnp.concatenate([_lane_tables(cos, sin, A_HEAD_DIM, 0, scale),
                           _lane_tables(cos, sin, A_HEAD_DIM, 0, 1.0)])
    wqk = w_qkv[:, :2 * qk_w].astype(BF16)
    wvt = w_qkv[:, 2 * qk_w:].T.astype(BF16)

    head_major = pl.BlockSpec((None, A_HEADS, tm, LANES), lambda bi, i: (bi, 0, i, 0))
    q, k, vt = pl.pallas_call(
        _proj_diff_kernel,
        grid=(b, s // tm),
        in_specs=[
            pl.BlockSpec((None, tm, d), lambda bi, i: (bi, i, 0)),
            _const_spec((1, d)),
            _const_spec((d, 2 * qk_w)),
            _const_spec((A_HEADS * A_V_DIM, d)),
            pl.BlockSpec((6, tm, LANES), lambda bi, i: (0, i, 0)),
        ],
        out_specs=[
            head_major,
            head_major,
            _vt_chunk_spec(A_HEADS, A_V_DIM, tm, tk),
        ],
        out_shape=[
            jax.ShapeDtypeStruct((b, A_HEADS, s, LANES), BF16),
            jax.ShapeDtypeStruct((b, A_HEADS, s, LANES), BF16),
            jax.ShapeDtypeStruct((b, A_HEADS, nck, A_V_DIM, tk), BF16),
        ],
        compiler_params=_params(("parallel", "parallel")),
        name="proj_diff",
    )(x, gain.reshape(1, d), wqk, wvt, tab)

    lam = jnp.stack([lam_q1, lam_k1, lam_q2, lam_k2]).astype(F32)
    per_head = pl.BlockSpec((None, None, s, LANES), lambda bi, h: (bi, h, 0, 0))
    return pl.pallas_call(
        functools.partial(_attn_diff_kernel, tq=tq, tk=tk, lambda_init=lambda_init),
        grid=(b, A_HEADS),
        in_specs=[
            _const_spec((4, A_HEAD_DIM)),
            _const_spec((1, A_V_DIM)),
            per_head,
            per_head,
            pl.BlockSpec((None, None, nck, A_V_DIM, tk), lambda bi, h: (bi, h, 0, 0, 0)),
        ],
        out_specs=pl.BlockSpec((None, s, A_V_DIM), lambda bi, h: (bi, 0, h)),
        out_shape=jax.ShapeDtypeStruct((b, s, A_HEADS * A_V_DIM), BF16),
        scratch_shapes=[pltpu.VMEM((2, nck, tk, tq), BF16)],
        compiler_params=_params(("parallel", "parallel")),
        name="attn_diff",
    )(lam, subln.reshape(1, A_V_DIM).astype(F32), q, k, vt)


def _proj_mla_kernel(x_ref, g_ref, waq_ref, wakv_ref, war_ref, qn_ref, kvn_ref, wqb_ref, wkb_ref,
                     wvt_ref, tab_ref, q_ref, k_ref, vt_ref):
    cw = vt_ref.shape[-1]
    r = B_ROPE // 2
    h = _rms_rows(x_ref[...], g_ref[...]).astype(BF16)
    q_lat = _rms_rows(jnp.dot(h, waq_ref[...], preferred_element_type=F32), qn_ref[...]).astype(BF16)
    kv_lat = _rms_rows(jnp.dot(h, wakv_ref[...], preferred_element_type=F32), kvn_ref[...]).astype(BF16)
    k_rope = _rope(jnp.dot(h, war_ref[...], preferred_element_type=F32), tab_ref, 3, r)
    for c in range(4):
        yq = jnp.dot(q_lat, wqb_ref[:, c * 512:(c + 1) * 512], preferred_element_type=F32)
        yk = jnp.dot(kv_lat, wkb_ref[:, c * 512:(c + 1) * 512], preferred_element_type=F32)
        for j in range(4):
            pair, lanes = 2 * c + j // 2, slice((j % 2) * LANES, (j % 2 + 1) * LANES)
            q_ref[pair, :, lanes] = _rope(yq[:, j * LANES:(j + 1) * LANES], tab_ref, 0, r).astype(BF16)
            k_ref[pair, :, lanes] = (yk[:, j * LANES:(j + 1) * LANES] + k_rope).astype(BF16)
    for half in range(2):
        vt = lax.dot_general(wvt_ref[half * 512:(half + 1) * 512, :], kv_lat, NT_DIMS,
                             preferred_element_type=F32)
        for hp in range(4):
            for c in range(vt_ref.shape[1]):
                vt_ref[half * 4 + hp, c] = vt[hp * LANES:(hp + 1) * LANES, c * cw:(c + 1) * cw].astype(BF16)


def _attn_mla_kernel(q_ref, k_ref, vt_ref, o_ref, *, tq, tk):
    lanes = [slice(e * LANES, (e + 1) * LANES) for e in range(2)]
    ones = jnp.ones((8, LANES), BF16)
    kmax = [_key_norm_bound(k_ref, lanes[e], ones, tq) for e in range(2)]

    def tile(row0, exact):
        chains = [(q_ref[pl.ds(row0, tq), lanes[e]], lanes[e], ()) for e in range(2)]
        shifts = None if exact else [_score_bound(chains[n][0], kmax[n]) for n in range(2)]
        (la, acca), (lb, accb) = _softmax_t(chains, k_ref, vt_ref, tk, shifts)
        o = jnp.concatenate([acca[0:B_V] * (1.0 / la), accb[B_V:2 * B_V] * (1.0 / lb)], axis=0).T
        o_ref[pl.ds(row0, tq), :] = o.astype(o_ref.dtype)
        return jnp.min(jnp.minimum(la, lb))

    _attend_tiles(tile, q_ref.shape[0] // tq, tq)


def _latent_attention(x, gain, w_a, q_norm, kv_norm, w_qb, w_kvb, cos, sin):
    b, s, d = x.shape
    tq = _tile(s, 512)
    tk = _tile(s, 4096)
    tm = _tile(s, 1024)
    nck = s // tk
    hw = B_HEADS * LANES
    npair = B_HEADS // 2
    scale = (B_NOPE + B_ROPE) ** -0.5 * LOG2E
    tab = jnp.concatenate([_lane_tables(cos, sin, LANES, B_NOPE, scale),
                           _lane_tables(cos, sin, LANES, B_NOPE, 1.0)])
    waq = w_a[:, :B_Q_RANK].astype(BF16)
    wakv = w_a[:, B_Q_RANK:B_Q_RANK + B_KV_RANK].astype(BF16)
    war = jnp.zeros((d, LANES), F32).at[:, B_NOPE:B_NOPE + B_ROPE].set(
        w_a[:, B_Q_RANK + B_KV_RANK:]).astype(BF16)
    pad = LANES - (B_NOPE + B_ROPE)
    wqb = jnp.pad(w_qb.reshape(B_Q_RANK, B_HEADS, B_NOPE + B_ROPE),
                  ((0, 0), (0, 0), (0, pad))).reshape(B_Q_RANK, hw).astype(BF16)
    wkv3 = w_kvb.reshape(B_KV_RANK, B_HEADS, B_NOPE + B_V)
    wkb = jnp.pad(wkv3[:, :, :B_NOPE], ((0, 0), (0, 0), (0, LANES - B_NOPE))
                  ).reshape(B_KV_RANK, hw).astype(BF16)
    wvt = wkv3[:, :, B_NOPE:].reshape(B_KV_RANK, B_HEADS * B_V).T.astype(BF16)

    pair_major = pl.BlockSpec((None, npair, tm, 2 * LANES), lambda bi, i: (bi, 0, i, 0))
    q, k, vt = pl.pallas_call(
        _proj_mla_kernel,
        grid=(b, s // tm),
        in_specs=[
            pl.BlockSpec((None, tm, d), lambda bi, i: (bi, i, 0)),
            _const_spec((1, d)),
            _const_spec((d, B_Q_RANK)),
            _const_spec((d, B_KV_RANK)),
            _const_spec((d, LANES)),
            _const_spec((1, B_Q_RANK)),
            _const_spec((1, B_KV_RANK)),
            _const_spec((B_Q_RANK, hw)),
            _const_spec((B_KV_RANK, hw)),
            _const_spec((B_HEADS * B_V, B_KV_RANK)),
            pl.BlockSpec((6, tm, LANES), lambda bi, i: (0, i, 0)),
        ],
        out_specs=[
            pair_major,
            pair_major,
            _vt_chunk_spec(npair, 2 * B_V, tm, tk),
        ],
        out_shape=[
            jax.ShapeDtypeStruct((b, npair, s, 2 * LANES), BF16),
            jax.ShapeDtypeStruct((b, npair, s, 2 * LANES), BF16),
            jax.ShapeDtypeStruct((b, npair, nck, 2 * B_V, tk), BF16),
        ],
        compiler_params=_params(("parallel", "parallel")),
        name="proj_mla",
    )(x, gain.reshape(1, d), waq, wakv, war, q_norm.reshape(1, -1), kv_norm.reshape(1, -1),
      wqb, wkb, wvt, tab)

    per_pair = pl.BlockSpec((None, None, s, 2 * LANES), lambda bi, h: (bi, h, 0, 0))
    return pl.pallas_call(
        functools.partial(_attn_mla_kernel, tq=tq, tk=tk),
        grid=(b, npair),
        in_specs=[
            per_pair,
            per_pair,
            pl.BlockSpec((None, None, nck, 2 * B_V, tk), lambda bi, h: (bi, h, 0, 0, 0)),
        ],
        out_specs=pl.BlockSpec((None, s, 2 * B_V), lambda bi, h: (bi, 0, h)),
        out_shape=jax.ShapeDtypeStruct((b, s, B_HEADS * B_V), BF16),
        compiler_params=_params(("parallel", "parallel")),
        name="attn_mla",
    )(q, k, vt)


def _proj_dil_kernel(x_ref, g_ref, w_ref, tab_ref, *refs):
    outs, scr = refs[:-1], refs[-1]
    tl = x_ref.shape[0]
    h = _rms_rows(x_ref[...], g_ref[...]).astype(BF16)
    width = C_HEADS * C_HEAD_DIM
    for g, (_, dil) in enumerate(C_GROUPS):
        for c in range(3 * width // 512):
            col = g * 3 * width + c * 512
            y = jnp.dot(h, w_ref[:, col:col + 512], preferred_element_type=F32)
            which = c // 2
            dst = outs[3 * g + which]
            lo = (c % 2) * 512
            for j in range(4):
                yj = y[:, j * LANES:(j + 1) * LANES]
                if which < 2:
                    yj = _rope(yj, tab_ref, 3 * which, C_ROT // 2)
                cols = slice(lo + j * LANES, lo + (j + 1) * LANES)
                if dil == 1:
                    dst[0, :, cols] = yj.astype(BF16)
                    continue
                scr[j] = yj
                for p in range(dil):
                    dst[p, :, cols] = scr[j, pl.ds(p, tl // dil, stride=dil), :].astype(BF16)


def _attn_dil_kernel(q_ref, kp_ref, kc_ref, kn_ref, vp_ref, vc_ref, vn_ref, o_ref, lse_ref, *, seq, tq):
    tt = q_ref.shape[0]
    nsub = tt // tq
    w = tq + 2 * C_HALO
    t0 = pl.program_id(2) * tt
    col = lax.broadcasted_iota(jnp.int32, (tq, w), 1)
    band = jnp.abs(col - C_HALO - lax.broadcasted_iota(jnp.int32, (tq, w), 0)) <= C_HALO
    head_lane = lax.broadcasted_iota(jnp.int32, (tq, LANES), 1)
    low = head_lane < C_HEAD_DIM
    cache = {}

    def window(prev_ref, cur_ref, next_ref, sub, lanes):
        lo, hi = sub * tq - C_HALO, (sub + 1) * tq + C_HALO
        parts = []
        if lo < 0:
            parts.append(prev_ref[:, lanes])
        parts.append(cur_ref[max(lo, 0):min(hi, tt), lanes])
        if hi > tt:
            parts.append(next_ref[:, lanes])
        return parts[0] if len(parts) == 1 else jnp.concatenate(parts, axis=0)

    def operands(sub, j):
        if (sub, j) not in cache:
            lanes = slice(j * LANES, (j + 1) * LANES)
            cache[(sub, j)] = (q_ref[sub * tq:(sub + 1) * tq, lanes],
                               window(kp_ref, kc_ref, kn_ref, sub, lanes),
                               window(vp_ref, vc_ref, vn_ref, sub, lanes))
        return cache[(sub, j)]

    def mask(sub):
        if ("mask", sub) not in cache:
            kpos = t0 + sub * tq - C_HALO + col
            cache[("mask", sub)] = band & (kpos >= 0) & (kpos < seq)
        return cache[("mask", sub)]

    def scores(unit):
        sub, j = unit
        q2 = operands(sub, j)[0]
        zero = jnp.zeros_like(q2)
        qm = jnp.concatenate([jnp.where(low, q2, zero), jnp.where(low, zero, q2)], axis=0)
        return lax.dot_general(qm, operands(sub, j)[1], NT_DIMS, preferred_element_type=F32)

    units = [(sub, j) for sub in range(nsub) for j in range(C_HEADS // 2)]
    sc = scores(units[0])
    lse_all = None
    for i, (sub, j) in enumerate(units):
        sc_next = scores(units[i + 1]) if i + 1 < len(units) else None
        msk = mask(sub)
        sc = jnp.where(jnp.concatenate([msk, msk], axis=0), sc, NEG_BIG)
        m = jnp.max(sc, axis=-1, keepdims=True)
        p = jnp.exp2(sc - m)
        l = jnp.sum(p, axis=-1, keepdims=True)
        pv = jnp.dot(p.astype(BF16), operands(sub, j)[2], preferred_element_type=F32) * (1.0 / l)
        lse = m + jnp.log2(l)
        rows = slice(sub * tq, (sub + 1) * tq)
        o_ref[rows, j * LANES:(j + 1) * LANES] = jnp.where(low, pv[0:tq], pv[tq:2 * tq]).astype(o_ref.dtype)
        for e in range(2):
            lse_e = lse[e * tq:(e + 1) * tq]
            lse_all = lse_e + jnp.zeros((tq, LANES), F32) if (j, e) == (0, 0) else jnp.where(
                head_lane == 2 * j + e, lse_e, lse_all)
        if j == C_HEADS // 2 - 1:
            lse_ref[rows, :] = lse_all
        sc = sc_next


def _combine_dil_kernel(o1_ref, o2_ref, o3_ref, l1_ref, l2_ref, l3_ref, o_ref, so_ref, sl_ref):
    rows = o_ref.shape[0]
    nlg = o_ref.shape[1] // LANES
    os_, ls = [], []
    for g, ((_, dil), oref, lref) in enumerate(zip(C_GROUPS, (o1_ref, o2_ref, o3_ref),
                                                   (l1_ref, l2_ref, l3_ref))):
        if dil == 1:
            os_.append(oref[0].astype(F32))
            ls.append(lref[0])
            continue
        for p in range(dil):
            og = oref[p].astype(F32)
            for j in range(nlg):
                so_ref[g - 1, j, pl.ds(p, rows // dil, stride=dil), :] = og[:, j * LANES:(j + 1) * LANES]
            sl_ref[g - 1, pl.ds(p, rows // dil, stride=dil), :] = lref[p]
        os_.append(jnp.concatenate([so_ref[g - 1, j] for j in range(nlg)], axis=-1))
        ls.append(sl_ref[g - 1])
    m = jnp.maximum(jnp.maximum(ls[0], ls[1]), ls[2])
    es = [jnp.exp2(l - m) for l in ls]
    inv = 1.0 / (es[0] + es[1] + es[2])
    width = C_HEADS * C_HEAD_DIM
    expand = (lax.broadcasted_iota(jnp.int32, (LANES, width), 1) // C_HEAD_DIM
              == lax.broadcasted_iota(jnp.int32, (LANES, width), 0)).astype(BF16)
    acc = jnp.zeros(o_ref.shape, F32)
    for e, og in zip(es, os_):
        wt = e * inv
        hi = wt.astype(BF16)
        lo = (wt - hi.astype(F32)).astype(BF16)
        wide = (jnp.dot(hi, expand, preferred_element_type=F32)
                + jnp.dot(lo, expand, preferred_element_type=F32))
        acc = acc + wide * og
    o_ref[...] = acc.astype(o_ref.dtype)


def _dilated_attention(x, gain, w_qkv, cos, sin):
    b, s, d = x.shape
    width = C_HEADS * C_HEAD_DIM
    scale = C_HEAD_DIM ** -0.5 * LOG2E
    tab = jnp.concatenate([_lane_tables(cos, sin, C_HEAD_DIM, 0, scale),
                           _lane_tables(cos, sin, C_HEAD_DIM, 0, 1.0)])
    dmax = max(dil for _, dil in C_GROUPS)
    for window, dil in C_GROUPS:
        assert (window // 2) // dil == C_HALO and (s // dil) % C_HALO == 0

    tl = _tile(s, 512)
    assert tl % (BF16_SUBLANES * dmax) == 0
    phase_spec = lambda dil, cols: pl.BlockSpec((None, dil, tl // dil, cols),
                                                lambda bi, i: (bi, 0, i, 0))
    qkv = pl.pallas_call(
        _proj_dil_kernel,
        grid=(b, s // tl),
        in_specs=[
            pl.BlockSpec((None, tl, d), lambda bi, i: (bi, i, 0)),
            _const_spec((1, d)),
            _const_spec((d, len(C_GROUPS) * 3 * width)),
            pl.BlockSpec((6, tl, LANES), lambda bi, i: (0, i, 0)),
        ],
        out_specs=[phase_spec(dil, width) for _, dil in C_GROUPS for _ in range(3)],
        out_shape=[jax.ShapeDtypeStruct((b, dil, s // dil, width), BF16)
                   for _, dil in C_GROUPS for _ in range(3)],
        scratch_shapes=[pltpu.VMEM((4, tl, LANES), F32)],
        compiler_params=_params(("parallel", "parallel")),
        name="proj_dil",
    )(x, gain.reshape(1, d), w_qkv.astype(BF16), tab)

    outs, lses = [], []
    for g, (_, dil) in enumerate(C_GROUPS):
        q, k, v = qkv[3 * g:3 * g + 3]
        seq = s // dil
        tt = _tile(seq, 512)
        tq = _tile(tt, 128)
        r = tt // C_HALO
        nhalo = seq // C_HALO
        cur = pl.BlockSpec((None, None, tt, width), lambda bi, p, i: (bi, p, i, 0))
        prev = pl.BlockSpec((None, None, C_HALO, width),
                            lambda bi, p, i, r=r: (bi, p, jnp.maximum(i * r - 1, 0), 0))
        nxt = pl.BlockSpec((None, None, C_HALO, width),
                           lambda bi, p, i, r=r, nhalo=nhalo: (bi, p, jnp.minimum((i + 1) * r, nhalo - 1), 0))
        o_g, lse_g = pl.pallas_call(
            functools.partial(_attn_dil_kernel, seq=seq, tq=tq),
            grid=(b, dil, seq // tt),
            in_specs=[cur, prev, cur, nxt, prev, cur, nxt],
            out_specs=[cur, pl.BlockSpec((None, None, tt, LANES), lambda bi, p, i: (bi, p, i, 0))],
            out_shape=[jax.ShapeDtypeStruct((b, dil, seq, width), BF16),
                       jax.ShapeDtypeStruct((b, dil, seq, LANES), F32)],
            compiler_params=_params(("parallel", "parallel", "parallel")),
            name=f"attn_dil{g}",
        )(q, k, k, k, v, v, v)
        outs.append(o_g)
        lses.append(lse_g)

    return pl.pallas_call(
        _combine_dil_kernel,
        grid=(b, s // tl),
        in_specs=([phase_spec(dil, width) for _, dil in C_GROUPS]
                  + [phase_spec(dil, LANES) for _, dil in C_GROUPS]),
        out_specs=pl.BlockSpec((None, tl, width), lambda bi, i: (bi, i, 0)),
        out_shape=jax.ShapeDtypeStruct((b, s, width), BF16),
        scratch_shapes=[pltpu.VMEM((len(C_GROUPS) - 1, width // LANES, tl, LANES), F32),
                        pltpu.VMEM((len(C_GROUPS) - 1, tl, LANES), F32)],
        compiler_params=_params(("parallel", "parallel")),
        name="combine_dil",
    )(*outs, *lses)


def _post_kernel(o_ref, x_ref, wo_ref, g_ref, wgu_ref, wout_ref, fg_ref, y_ref, acc_ref, *,
                 apply_final):
    x1 = x_ref[...] + jnp.dot(o_ref[...], wo_ref[...], preferred_element_type=F32)
    h = _rms_rows(x1, g_ref[...]).astype(BF16)
    acc_ref[...] = x1
    for c in range(FFN_DIM // FFN_CHUNK):
        lo = c * FFN_CHUNK
        gate = jnp.dot(h, wgu_ref[:, lo:lo + FFN_CHUNK], preferred_element_type=F32)
        up = jnp.dot(h, wgu_ref[:, FFN_DIM + lo:FFN_DIM + lo + FFN_CHUNK], preferred_element_type=F32)
        act = (gate * jax.nn.sigmoid(gate) * up).astype(BF16)
        acc_ref[...] += jnp.dot(act, wout_ref[lo:lo + FFN_CHUNK, :], preferred_element_type=F32)
    y = acc_ref[...]
    if apply_final:
        y = _rms_rows(y, fg_ref[...])
    y_ref[...] = y


def _post(o, x, w_o, ffn_gain, w_gu, w_out, final_gain, apply_final):
    b, s, d = x.shape
    m = b * s
    tm = _tile(m, 1024)
    row = lambda i: (i, 0)
    y = pl.pallas_call(
        functools.partial(_post_kernel, apply_final=apply_final),
        grid=(m // tm,),
        in_specs=[
            pl.BlockSpec((tm, d), row),
            pl.BlockSpec((tm, d), row),
            _const_spec((d, d)),
            _const_spec((1, d)),
            _const_spec((d, 2 * FFN_DIM)),
            _const_spec((FFN_DIM, d)),
            _const_spec((1, d)),
        ],
        out_specs=pl.BlockSpec((tm, d), row),
        out_shape=jax.ShapeDtypeStruct((m, d), F32),
        scratch_shapes=[pltpu.VMEM((tm, d), F32)],
        compiler_params=_params(("parallel",)),
        name="post",
    )(o.reshape(m, d), x.reshape(m, d), w_o.astype(BF16), ffn_gain.reshape(1, d),
      w_gu.astype(BF16), w_out.astype(BF16), final_gain.reshape(1, d))
    return y.reshape(b, s, d)


def kernel(x, attn_norm, ffn_norm, final_norm, a_w_qkv, a_lambda_q1, a_lambda_k1, a_lambda_q2,
           a_lambda_k2, a_subln, a_w_o, b_w_a, b_q_norm, b_kv_norm, b_w_qb, b_w_kvb, b_w_o,
           c_w_qkv, c_w_o, f_w_gu, f_w_out):
    depth = attn_norm.shape[0]
    s = x.shape[1]
    cos_p, sin_p = _rope_tables(s, A_ROT)
    cos_b, sin_b = _rope_tables(s, B_ROPE)
    for i in range(depth):
        mixer, j = i % N_MIXERS, i // N_MIXERS
        if mixer == 0:
            lambda_init = 0.8 - 0.6 * math.exp(-0.3 * i)
            o = _diff_attention(x, attn_norm[i], a_w_qkv[j], a_lambda_q1[j], a_lambda_k1[j],
                                a_lambda_q2[j], a_lambda_k2[j], a_subln[j], lambda_init, cos_p, sin_p)
            w_o = a_w_o[j]
        elif mixer == 1:
            o = _latent_attention(x, attn_norm[i], b_w_a[j], b_q_norm[j], b_kv_norm[j], b_w_qb[j],
                                  b_w_kvb[j], cos_b, sin_b)
            w_o = b_w_o[j]
        else:
            o = _dilated_attention(x, attn_norm[i], c_w_qkv[j], cos_p, sin_p)
            w_o = c_w_o[j]
        x = _post(o, x, w_o, ffn_norm[i], f_w_gu[i], f_w_out[i], final_norm, i == depth - 1)
    return x
```
